```python
import math
import jax, jax.numpy as jnp
from jax import lax
import numpy as np


D_MODEL = 1024
BATCH = 8
SEQ = 4096
DEPTH = 2
DEC_BATCH = 32
DEC_SEQ = 1
PAST_LEN = 16384
PAGE_SIZE = 128

PLE_DIM = 256
RMS_EPS = 1e-6
Q_BLOCK = 128
ML_HEADS = 4
ML_DH = 128
ML_W = ML_HEADS * ML_DH
ML_CHUNK = 64
SB_HEADS = 8
SB_DH = 64
SB_W = SB_HEADS * SB_DH
DF_HEADS = 8
DF_D = 64
DF_V = 2 * DF_D
DF_W = DF_HEADS * DF_V
DF_LAMBDA_INIT = 0.8 - 0.6 * math.exp(-0.3 * 1)
PEER_HEADS = 8
PEER_NKEYS = 128
PEER_EXPERTS = PEER_NKEYS * PEER_NKEYS
PEER_DKEY = 256
PEER_TOPK = 16
PEER_BLOCK = 256

kernel_name = 'hybrid_mlstm_stickbreak_diffattn_peer_step'


def rmsnorm(x, w):
    xf = x.astype(jnp.float32)
    y = xf * lax.rsqrt(jnp.mean(xf * xf, axis=-1, keepdims=True) + RMS_EPS)
    return (y * w.astype(jnp.float32)).astype(x.dtype)


def gather_pages(cache, page_table):
    g = cache[page_table]
    return g.reshape((page_table.shape[0], page_table.shape[1] * PAGE_SIZE) + cache.shape[2:])


def over_query_blocks(fn, q, q_pos):
    B, Sq = q.shape[:2]
    bq = math.gcd(Sq, Q_BLOCK)
    nb = Sq // bq
    qb = jnp.moveaxis(q.reshape((B, nb, bq) + q.shape[2:]), 1, 0)
    pb = q_pos.reshape(nb, bq)
    out = lax.map(lambda a: fn(a[0], a[1]), (qb, pb))
    return jnp.moveaxis(out, 0, 1).reshape((B, Sq) + out.shape[3:])


def weighted_values(A, v_segs, spec):
    out, off = 0.0, 0
    for v in v_segs:
        n = v.shape[1]
        out = out + jnp.einsum(spec, A[..., off:off + n].astype(v.dtype), v)
        off += n
    return out


def sb_block(qi, pi, k_segs, v_segs, kpos):
    z = jnp.concatenate([jnp.einsum('bqhd,bkhd->bhqk', qi, k) for k in k_segs], axis=-1).astype(jnp.float32)
    mask = kpos[None, :] < pi[:, None]
    log_beta = jax.nn.log_sigmoid(z)
    log_1m = jnp.where(mask, jax.nn.log_sigmoid(-z), 0.0)
    rev = lax.cumsum(log_1m, axis=3, reverse=True)
    rev_excl = jnp.concatenate([rev[..., 1:], jnp.zeros_like(rev[..., :1])], axis=-1)
    A = jnp.where(mask, jnp.exp(log_beta + rev_excl), 0.0)
    return weighted_values(A, v_segs, 'bhqk,bkhd->bqhd')


def diff_block(qi, pi, k_segs, v_segs, kpos, lam):
    z = jnp.concatenate([jnp.einsum('bqhcd,bkhcd->bhcqk', qi, k) for k in k_segs], axis=-1).astype(jnp.float32)
    mask = kpos[None, :] <= pi[:, None]
    a = jax.nn.softmax(jnp.where(mask, z, -jnp.inf), axis=-1)
    A = a[:, :, 0] - lam * a[:, :, 1]
    return weighted_values(A, v_segs, 'bhqk,bkhe->bqhe')


def mlstm(q, k, v, ig, fg, C0, n0, m0):
    B, S, H, dh = q.shape
    f32 = jnp.float32
    L = math.gcd(S, ML_CHUNK)
    nc = S // L

    def chunks(t):
        t = t.astype(f32).reshape((B, nc, L, H) + t.shape[3:])
        return jnp.moveaxis(t, (1, 3), (0, 2))

    qs = chunks(q) * (dh ** -0.5)
    ks = chunks(k)
    vs = chunks(v)
    li = chunks(ig)
    lf = chunks(jax.nn.log_sigmoid(fg.astype(f32)))
    tril = jnp.tril(jnp.ones((L, L), dtype=bool))

    def step(carry, inp):
        C, n, m = carry
        qc, kc, vc, lic, lfc = inp
        b = jnp.cumsum(lfc, axis=-1)
        dmat = jnp.where(tril, b[..., :, None] - b[..., None, :] + lic[..., None, :], -jnp.inf)
        inter = b + m[..., None]
        m_t = jnp.maximum(inter, jnp.max(dmat, axis=-1))
        w_inter = jnp.exp(inter - m_t)
        wqk = jnp.exp(dmat - m_t[..., None]) * jnp.einsum('bhtd,bhsd->bhts', qc, kc)
        num = w_inter[..., None] * jnp.einsum('bhvd,bhtd->bhtv', C, qc) + jnp.einsum('bhts,bhsv->bhtv', wqk, vc)
        den = w_inter * jnp.einsum('bhd,bhtd->bht', n, qc) + jnp.sum(wqk, axis=-1)
        h = num / jnp.maximum(jnp.abs(den), jnp.exp(-m_t))[..., None]
        g = b[..., -1:] - b + lic
        m_new = jnp.maximum(b[..., -1] + m, jnp.max(g, axis=-1))
        a = jnp.exp(b[..., -1] + m - m_new)
        wg = jnp.exp(g - m_new[..., None])
        C_new = a[..., None, None] * C + jnp.einsum('bhsv,bhsd->bhvd', wg[..., None] * vc, kc)
        n_new = a[..., None] * n + jnp.einsum('bhs,bhsd->bhd', wg, kc)
        return (C_new, n_new, m_new), h

    (C, n, m), h = lax.scan(step, (C0.astype(f32), n0.astype(f32), m0.astype(f32)), (qs, ks, vs, li, lf))
    h = jnp.moveaxis(h, (0, 2), (1, 3)).reshape(B, S, H, dh)
    return h.astype(q.dtype), (C, n, m)


def peer(x, wq, keys, u, v):
    B, S, D = x.shape
    T = B * S
    tb = math.gcd(T, PEER_BLOCK)
    xb = x.reshape(T // tb, tb, D)
    K = PEER_TOPK

    def block(xt):
        q = (xt @ wq).reshape(tb, PEER_HEADS, 2, PEER_DKEY // 2)
        s = jnp.einsum('thcd,cnd->thcn', q, keys).astype(jnp.float32)
        s_top, i_top = lax.top_k(s, K)
        cand = (s_top[:, :, 0, :, None] + s_top[:, :, 1, None, :]).reshape(tb, PEER_HEADS, K * K)
        c_top, c_idx = lax.top_k(cand, K)
        i1 = jnp.take_along_axis(i_top[:, :, 0], c_idx // K, axis=-1)
        i2 = jnp.take_along_axis(i_top[:, :, 1], c_idx % K, axis=-1)
        e = i1 * PEER_NKEYS + i2
        g = jax.nn.softmax(c_top, axis=-1)
        act = jax.nn.gelu(jnp.einsum('thkd,td->thk', u[e], xt).astype(jnp.float32))
        return jnp.einsum('thk,thkd->td', (g * act).astype(xt.dtype), v[e])

    return lax.map(block, xb).reshape(B, S, D)


def setup_inputs(seed: int = 0) -> dict:
    key = jax.random.key(seed)
    ks = iter(jax.random.split(key, 64))
    f32 = jnp.float32

    def nrm(shape, scale=1.0):
        return jax.random.normal(next(ks), shape, f32) * scale

    def gain(shape):
        return 1.0 + nrm(shape, 0.02)

    n_pages = PAST_LEN // PAGE_SIZE
    n_used = DEC_BATCH * n_pages
    n_pool = n_used + n_used // 4
    perm = jax.random.permutation(next(ks), n_pool)
    page_table = perm[:n_used].reshape(DEC_BATCH, n_pages).astype(jnp.int32)
    sd = D_MODEL ** -0.5
    in0_cols = 4 * ML_W + 2 * ML_HEADS + 3 * SB_W

    inp = {}
    inp['x_prompt'] = nrm((BATCH, SEQ, D_MODEL))
    inp['x_sample'] = nrm((DEC_BATCH, DEC_SEQ, D_MODEL))
    inp['p_prompt'] = nrm((DEPTH, BATCH, SEQ, PLE_DIM))
    inp['p_sample'] = nrm((DEPTH, DEC_BATCH, DEC_SEQ, PLE_DIM))
    inp['state_mlstm_C'] = nrm((DEC_BATCH, ML_HEADS, ML_DH, ML_DH), 0.1)
    inp['state_mlstm_n'] = nrm((DEC_BATCH, ML_HEADS, ML_DH), 0.5)
    inp['state_mlstm_m'] = nrm((DEC_BATCH, ML_HEADS))
    inp['cache_sb_k'] = nrm((n_pool, PAGE_SIZE, SB_HEADS, SB_DH))
    inp['cache_sb_v'] = nrm((n_pool, PAGE_SIZE, SB_HEADS, SB_DH))
    inp['cache_diff_k'] = nrm((n_pool, PAGE_SIZE, DF_HEADS, 2 * DF_D))
    inp['cache_diff_v'] = nrm((n_pool, PAGE_SIZE, DF_HEADS, DF_V))
    inp['page_table'] = page_table
    inp['norm_mix0'] = gain((D_MODEL,))
    inp['w_in0'] = nrm((D_MODEL, in0_cols), sd)
    inp['b_igate0'] = nrm((ML_HEADS,), 0.1)
    inp['b_fgate0'] = 3.0 + nrm((ML_HEADS,), 0.1)
    inp['mlstm_norm0'] = gain((ML_HEADS, ML_DH))
    inp['w_out0'] = nrm((ML_W + SB_W, D_MODEL), (ML_W + SB_W) ** -0.5)
    inp['norm_ffn0'] = gain((D_MODEL,))
    inp['peer_wq0'] = nrm((D_MODEL, PEER_HEADS * PEER_DKEY), sd)
    inp['peer_keys0'] = nrm((2, PEER_NKEYS, PEER_DKEY // 2), (PEER_DKEY // 2) ** -0.5)
    inp['peer_u0'] = nrm((PEER_EXPERTS, D_MODEL), sd)
    inp['peer_v0'] = nrm((PEER_EXPERTS, D_MODEL), 0.25)
    inp['ple_gate0'] = nrm((D_MODEL, D_MODEL), sd)
    inp['ple_proj0'] = nrm((PLE_DIM, D_MODEL), PLE_DIM ** -0.5)
    inp['norm_mix1'] = gain((D_MODEL,))
    inp['w_qkv1'] = nrm((D_MODEL, 3 * DF_W), sd)
    inp['lambda_q1'] = nrm((DF_D,), 0.1)
    inp['lambda_k1'] = nrm((DF_D,), 0.1)
    inp['lambda_q2'] = nrm((DF_D,), 0.1)
    inp['lambda_k2'] = nrm((DF_D,), 0.1)
    inp['diff_norm1'] = gain((DF_V,))
    inp['w_out1'] = nrm((DF_W, D_MODEL), DF_W ** -0.5)
    inp['norm_ffn1'] = gain((D_MODEL,))
    inp['peer_wq1'] = nrm((D_MODEL, PEER_HEADS * PEER_DKEY), sd)
    inp['peer_keys1'] = nrm((2, PEER_NKEYS, PEER_DKEY // 2), (PEER_DKEY // 2) ** -0.5)
    inp['peer_u1'] = nrm((PEER_EXPERTS, D_MODEL), sd)
    inp['peer_v1'] = nrm((PEER_EXPERTS, D_MODEL), 0.25)
    inp['ple_gate1'] = nrm((D_MODEL, D_MODEL), sd)
    inp['ple_proj1'] = nrm((PLE_DIM, D_MODEL), PLE_DIM ** -0.5)
    inp['final_norm'] = gain((D_MODEL,))
    return inp


def reference(x_prompt, x_sample, p_prompt, p_sample, state_mlstm_C, state_mlstm_n, state_mlstm_m,
              cache_sb_k, cache_sb_v, cache_diff_k, cache_diff_v, page_table,
              norm_mix0, w_in0, b_igate0, b_fgate0, mlstm_norm0, w_out0, norm_ffn0,
              peer_wq0, peer_keys0, peer_u0, peer_v0, ple_gate0, ple_proj0,
              norm_mix1, w_qkv1, lambda_q1, lambda_k1, lambda_q2, lambda_k2, diff_norm1, w_out1, norm_ffn1,
              peer_wq1, peer_keys1, peer_u1, peer_v1, ple_gate1, ple_proj1, final_norm):
    f32 = jnp.float32
    norm_mix = (norm_mix0, norm_mix1)
    norm_ffn = (norm_ffn0, norm_ffn1)
    peer_params = ((peer_wq0, peer_keys0, peer_u0, peer_v0), (peer_wq1, peer_keys1, peer_u1, peer_v1))
    ple_gate = (ple_gate0, ple_gate1)
    ple_proj = (ple_proj0, ple_proj1)
    past_pos = jnp.arange(PAST_LEN, dtype=jnp.int32)
    in0_sizes = (ML_W, ML_W, ML_W, ML_W, ML_HEADS, ML_HEADS, SB_W, SB_W, SB_W)
    in0_cuts = [int(c) for c in np.cumsum(in0_sizes)[:-1]]
    lam = (jnp.exp(jnp.sum((lambda_q1 * lambda_k1).astype(f32)))
           - jnp.exp(jnp.sum((lambda_q2 * lambda_k2).astype(f32))) + DF_LAMBDA_INIT)

    def mixer_ab(xn, q_pos, ml_state, paged):
        B, S, _ = xn.shape
        mq, mk, mv, mo, mi, mf, sq, sk, sv = jnp.split(xn @ w_in0, in0_cuts, axis=-1)
        h_ml, ml_new = mlstm(mq.reshape(B, S, ML_HEADS, ML_DH), mk.reshape(B, S, ML_HEADS, ML_DH),
                             mv.reshape(B, S, ML_HEADS, ML_DH), mi + b_igate0, mf + b_fgate0, *ml_state)
        h_ml = jax.nn.sigmoid(mo) * rmsnorm(h_ml, mlstm_norm0).reshape(B, S, ML_W)
        sq = sq.reshape(B, S, SB_HEADS, SB_DH) * (SB_DH ** -0.5)
        sk = sk.reshape(B, S, SB_HEADS, SB_DH)
        sv = sv.reshape(B, S, SB_HEADS, SB_DH)
        if paged:
            k_segs = [gather_pages(cache_sb_k, page_table), sk]
            v_segs = [gather_pages(cache_sb_v, page_table), sv]
            kpos = jnp.concatenate([past_pos, q_pos])
        else:
            k_segs, v_segs, kpos = [sk], [sv], q_pos
        h_sb = over_query_blocks(lambda qi, pi: sb_block(qi, pi, k_segs, v_segs, kpos), sq, q_pos)
        out = jnp.concatenate([h_ml, h_sb.reshape(B, S, SB_W)], axis=-1) @ w_out0
        return out, ml_new, (sk, sv)

    def mixer_c(xn, q_pos, paged):
        B, S, _ = xn.shape
        q, k, v = jnp.split(xn @ w_qkv1, 3, axis=-1)
        q = q.reshape(B, S, DF_HEADS, 2, DF_D) * (DF_D ** -0.5)
        k = k.reshape(B, S, DF_HEADS, 2, DF_D)
        v = v.reshape(B, S, DF_HEADS, DF_V)
        if paged:
            kp = gather_pages(cache_diff_k, page_table)
            kp = kp.reshape(kp.shape[:3] + (2, DF_D))
            k_segs = [kp, k]
            v_segs = [gather_pages(cache_diff_v, page_table), v]
            kpos = jnp.concatenate([past_pos, q_pos])
        else:
            k_segs, v_segs, kpos = [k], [v], q_pos
        o = over_query_blocks(lambda qi, pi: diff_block(qi, pi, k_segs, v_segs, kpos, lam), q, q_pos)
        o = rmsnorm(o, diff_norm1) * (1.0 - DF_LAMBDA_INIT)
        return o.reshape(B, S, DF_W) @ w_out1, (k.reshape(B, S, DF_HEADS, 2 * DF_D), v)

    def run_group(x, p, q_pos, ml_state, paged):
        h = x
        for i in range(DEPTH):
            hn = rmsnorm(h, norm_mix[i])
            if i % 2 == 0:
                mix, ml_new, sb_kv = mixer_ab(hn, q_pos, ml_state, paged)
            else:
                mix, df_kv = mixer_c(hn, q_pos, paged)
            h = h + mix
            h = h + peer(rmsnorm(h, norm_ffn[i]), *peer_params[i])
            h = h + jax.nn.sigmoid(h @ ple_gate[i]) * (p[i] @ ple_proj[i])
        return rmsnorm(h, final_norm), ml_new, sb_kv, df_kv

    Bp = x_prompt.shape[0]
    ml_zero = (jnp.zeros((Bp, ML_HEADS, ML_DH, ML_DH), f32), jnp.zeros((Bp, ML_HEADS, ML_DH), f32),
               jnp.zeros((Bp, ML_HEADS), f32))
    pos_p = jnp.arange(x_prompt.shape[1], dtype=jnp.int32)
    y_prompt, (C_p, n_p, m_p), (sbk_p, sbv_p), (dfk_p, dfv_p) = run_group(x_prompt, p_prompt, pos_p, ml_zero, False)

    pos_s = PAST_LEN + jnp.arange(x_sample.shape[1], dtype=jnp.int32)
    y_sample, (C_s, n_s, m_s), (sbk_s, sbv_s), (dfk_s, dfv_s) = run_group(
        x_sample, p_sample, pos_s, (state_mlstm_C, state_mlstm_n, state_mlstm_m), True)

    return (y_prompt, y_sample, C_p, n_p, m_p, sbk_p, sbv_p, dfk_p, dfv_p,
            C_s, n_s, m_s, sbk_s, sbv_s, dfk_s, dfv_s)
```

```python
import functools
import math

import jax
import jax.numpy as jnp
from jax import lax
from jax.experimental import pallas as pl
from jax.experimental.pallas import tpu as pltpu

F32 = jnp.float32
BF16 = jnp.bfloat16
I32 = jnp.int32

RMS_EPS = 1e-6
LANES = 128
ML_HEADS, ML_DH, ML_CHUNK = 4, 128, 64
SB_HEADS, SB_DH = 8, 64
DF_HEADS, DF_D, DF_V = 8, 64, 128
DF_LAMBDA_INIT = 0.8 - 0.6 * math.exp(-0.3 * 1)
PEER_HEADS, PEER_NKEYS, PEER_TOPK = 8, 128, 16
PAGE = 128
VMEM_LIMIT = 56 * 1024 * 1024
NEG_INF = float("-inf")


def _params(sem):
    return pltpu.CompilerParams(dimension_semantics=sem, vmem_limit_bytes=VMEM_LIMIT)


def _bdot(a, b):
    return jnp.dot(a.astype(BF16), b.astype(BF16), preferred_element_type=F32)


def _bdot_nt(a, b):
    return lax.dot_general(a.astype(BF16), b.astype(BF16), (((1,), (1,)), ((), ())),
                           preferred_element_type=F32)


def _rms(x, w):
    return x * lax.rsqrt(jnp.mean(x * x, axis=-1, keepdims=True) + RMS_EPS) * w


def _row_tile(t):
    for tm in (256, 128, 64, 32, 16, 8):
        if t % tm == 0:
            return tm
    raise ValueError(f"row count {t} must be a multiple of 8")


def _norm_proj_kernel(x_ref, nw_ref, w_ref, *o_refs, splits):
    xn = _rms(x_ref[...], nw_ref[...]).astype(BF16)
    off = 0
    for o_ref, n in zip(o_refs, splits):
        o_ref[...] = jnp.dot(xn, w_ref[:, off:off + n], preferred_element_type=F32)
        off += n


def norm_proj(x, nw, w_bf16, splits):
    t, d = x.shape
    n = w_bf16.shape[1]
    assert sum(splits) == n and all(s % LANES == 0 for s in splits)
    tm = _row_tile(t)
    return pl.pallas_call(
        functools.partial(_norm_proj_kernel, splits=tuple(splits)),
        grid=(t // tm,),
        in_specs=[pl.BlockSpec((tm, d), lambda i: (i, 0)),
                  pl.BlockSpec((1, d), lambda i: (0, 0)),
                  pl.BlockSpec((d, n), lambda i: (0, 0))],
        out_specs=[pl.BlockSpec((tm, s), lambda i: (i, 0)) for s in splits],
        out_shape=[jax.ShapeDtypeStruct((t, s), F32) for s in splits],
        compiler_params=_params(("arbitrary",)),
        name="norm_proj",
    )(x, nw.reshape(1, d), w_bf16)


def _proj_res_kernel(*refs, n_in):
    res_ref = refs[0]
    a_refs = refs[1:1 + n_in]
    w_refs = refs[1 + n_in:1 + 2 * n_in]
    o_ref = refs[1 + 2 * n_in]
    acc = res_ref[...]
    for a_ref, w_ref in zip(a_refs, w_refs):
        acc = acc + jnp.dot(a_ref[...].astype(BF16), w_ref[...], preferred_element_type=F32)
    o_ref[...] = acc


def proj_residual(res, acts, ws_bf16):
    t, d = res.shape
    tm = _row_tile(t)
    n_in = len(acts)
    in_specs = [pl.BlockSpec((tm, d), lambda i: (i, 0))]
    in_specs += [pl.BlockSpec((tm, a.shape[1]), lambda i: (i, 0)) for a in acts]
    in_specs += [pl.BlockSpec(w.shape, lambda i: (0, 0)) for w in ws_bf16]
    return pl.pallas_call(
        functools.partial(_proj_res_kernel, n_in=n_in),
        grid=(t // tm,),
        in_specs=in_specs,
        out_specs=pl.BlockSpec((tm, d), lambda i: (i, 0)),
        out_shape=jax.ShapeDtypeStruct((t, d), F32),
        compiler_params=_params(("arbitrary",)),
        name="proj_residual",
    )(res, *acts, *ws_bf16)


def _ple_kernel(h_ref, p_ref, wg_ref, wp_ref, fn_ref, o_ref, *, final_norm):
    h = h_ref[...]
    gate = jax.nn.sigmoid(jnp.dot(h.astype(BF16), wg_ref[...], preferred_element_type=F32))
    emb = jnp.dot(p_ref[...].astype(BF16), wp_ref[...], preferred_element_type=F32)
    y = h + gate * emb
    if final_norm:
        y = _rms(y, fn_ref[...])
    o_ref[...] = y


def ple(h, p, wg_bf16, wp_bf16, fnorm, final_norm):
    t, d = h.shape
    pd = p.shape[1]
    tm = _row_tile(t)
    return pl.pallas_call(
        functools.partial(_ple_kernel, final_norm=final_norm),
        grid=(t // tm,),
        in_specs=[pl.BlockSpec((tm, d), lambda i: (i, 0)),
                  pl.BlockSpec((tm, pd), lambda i: (i, 0)),
                  pl.BlockSpec((d, d), lambda i: (0, 0)),
                  pl.BlockSpec((pd, d), lambda i: (0, 0)),
                  pl.BlockSpec((1, d), lambda i: (0, 0))],
        out_specs=pl.BlockSpec((tm, d), lambda i: (i, 0)),
        out_shape=jax.ShapeDtypeStruct((t, d), F32),
        compiler_params=_params(("arbitrary",)),
        name="ple",
    )(h, p, wg_bf16, wp_bf16, fnorm.reshape(1, d))


def _mlstm_chunk_kernel(q_ref, k_ref, v_ref, o_ref, g_ref, gb_ref, nw_ref,
                        h_ref, c_out, n_out, m_out, c_sc, n_sc, m_sc):
    ci = pl.program_id(1)
    L = ML_CHUNK

    @pl.when(ci == 0)
    def _():
        c_sc[...] = jnp.zeros_like(c_sc)
        n_sc[...] = jnp.zeros_like(n_sc)
        m_sc[...] = jnp.zeros_like(m_sc)

    hp = lax.Precision.HIGHEST
    g = g_ref[...] + gb_ref[...]
    lf = jax.nn.log_sigmoid(g)
    ri = lax.broadcasted_iota(I32, (L, L), 0)
    cj = lax.broadcasted_iota(I32, (L, L), 1)
    tril = (cj <= ri)
    low = tril.astype(F32)
    upp = (ri <= cj).astype(F32)
    eye = (lax.broadcasted_iota(I32, (LANES, LANES), 0) ==
           lax.broadcasted_iota(I32, (LANES, LANES), 1)).astype(F32)
    nt = (((1,), (1,)), ((), ()))
    g_t = lax.dot_general(eye, g, nt, precision=hp, preferred_element_type=F32)
    lf_t = lax.dot_general(eye, lf, nt, precision=hp, preferred_element_type=F32)
    b_c = jnp.dot(low, lf, precision=hp, preferred_element_type=F32)
    b_r = jnp.dot(lf_t, upp, precision=hp, preferred_element_type=F32)

    for h in range(ML_HEADS):
        sl = slice(h * ML_DH, (h + 1) * ML_DH)
        q = q_ref[:, sl] * (ML_DH ** -0.5)
        k = k_ref[:, sl]
        v = v_ref[:, sl]
        li_c = g[:, h:h + 1]
        li_r = g_t[h:h + 1, :]
        bc = b_c[:, ML_HEADS + h:ML_HEADS + h + 1]
        br = b_r[ML_HEADS + h:ML_HEADS + h + 1, :]
        m_prev = m_sc[:, h:h + 1]
        c_prev = c_sc[h]
        n_prev = n_sc[h:h + 1, :]
        dmat = jnp.where(tril, bc - br + li_r, NEG_INF)
        inter = bc + m_prev
        m_t = jnp.maximum(inter, jnp.max(dmat, axis=1, keepdims=True))
        w_inter = jnp.exp(inter - m_t)
        wqk = jnp.exp(dmat - m_t) * _bdot_nt(q, k)
        num = w_inter * _bdot_nt(q, c_prev) + _bdot(wqk, v)
        den = w_inter * jnp.sum(q * n_prev, axis=1, keepdims=True) + jnp.sum(wqk, axis=1, keepdims=True)
        hh = num / jnp.maximum(jnp.abs(den), jnp.exp(-m_t))
        b_last = br[:, L - 1:L]
        g_r = b_last - br + li_r
        g_c = b_last - bc + li_c
        m_new = jnp.maximum(b_last + m_prev, jnp.max(g_r, axis=1, keepdims=True))
        a = jnp.exp(b_last + m_prev - m_new)
        wg_c = jnp.exp(g_c - m_new)
        wv = (wg_c * v).astype(BF16)
        c_new = a * c_prev + lax.dot_general(wv, k.astype(BF16), (((0,), (0,)), ((), ())),
                                             preferred_element_type=F32)
        n_new = a * n_prev + jnp.sum(wg_c * k, axis=0, keepdims=True)
        c_sc[h] = c_new
        n_sc[h:h + 1, :] = n_new
        m_sc[:, h:h + 1] = m_new
        hn = _rms(hh, nw_ref[:, sl])
        h_ref[:, sl] = jax.nn.sigmoid(o_ref[:, sl]) * hn

    @pl.when(ci == pl.num_programs(1) - 1)
    def _():
        c_out[0] = c_sc[...]
        n_out[0] = n_sc[...]
        m_out[0] = m_sc[:, 0:ML_HEADS]


def mlstm_prompt(ml, gates, gate_bias, norm_w, batch, seq):
    L = ML_CHUNK
    nc = seq // L
    w = ML_HEADS * ML_DH
    col = lambda j: pl.BlockSpec((L, w), lambda b, c: (b * nc + c, j))
    return pl.pallas_call(
        _mlstm_chunk_kernel,
        grid=(batch, nc),
        in_specs=[col(0), col(1), col(2), col(3),
                  pl.BlockSpec((L, LANES), lambda b, c: (b * nc + c, 0)),
                  pl.BlockSpec((1, LANES), lambda b, c: (0, 0)),
                  pl.BlockSpec((1, w), lambda b, c: (0, 0))],
        out_specs=[pl.BlockSpec((L, w), lambda b, c: (b * nc + c, 0)),
                   pl.BlockSpec((1, ML_HEADS, ML_DH, ML_DH), lambda b, c: (b, 0, 0, 0)),
                   pl.BlockSpec((1, ML_HEADS, ML_DH), lambda b, c: (b, 0, 0)),
                   pl.BlockSpec((1, 1, ML_HEADS), lambda b, c: (b, 0, 0))],
        out_shape=[jax.ShapeDtypeStruct((batch * seq, w), F32),
                   jax.ShapeDtypeStruct((batch, ML_HEADS, ML_DH, ML_DH), F32),
                   jax.ShapeDtypeStruct((batch, ML_HEADS, ML_DH), F32),
                   jax.ShapeDtypeStruct((batch, 1, ML_HEADS), F32)],
        scratch_shapes=[pltpu.VMEM((ML_HEADS, ML_DH, ML_DH), F32),
                        pltpu.VMEM((ML_HEADS, ML_DH), F32),
                        pltpu.VMEM((1, LANES), F32)],
        compiler_params=_params(("arbitrary", "arbitrary")),
        name="mlstm_chunk",
    )(ml, ml, ml, ml, gates, gate_bias, norm_w.reshape(1, w))


def _mlstm_step_kernel(x_ref, g_ref, gb_ref, nw_ref, c_ref, n_ref, m_ref,
                       h_ref, c_out, n_out, m_out):
    w = ML_HEADS * ML_DH
    g = g_ref[0] + gb_ref[...]
    lf = jax.nn.log_sigmoid(g)
    eye = (lax.broadcasted_iota(I32, (ML_DH, ML_DH), 0) ==
           lax.broadcasted_iota(I32, (ML_DH, ML_DH), 1)).astype(F32)

    def to_col(row):
        return jnp.sum(eye * row, axis=1, keepdims=True)

    def to_row(colv):
        return jnp.sum(eye * colv, axis=0, keepdims=True)

    x = x_ref[0]
    for h in range(ML_HEADS):
        q = x[:, h * ML_DH:(h + 1) * ML_DH] * (ML_DH ** -0.5)
        k = x[:, w + h * ML_DH:w + (h + 1) * ML_DH]
        v = x[:, 2 * w + h * ML_DH:2 * w + (h + 1) * ML_DH]
        o = x[:, 3 * w + h * ML_DH:3 * w + (h + 1) * ML_DH]
        li = g[:, h:h + 1]
        lfh = lf[:, ML_HEADS + h:ML_HEADS + h + 1]
        m_prev = m_ref[0][:, h:h + 1]
        c_prev = c_ref[0, h]
        n_prev = n_ref[0][h:h + 1, :]
        inter = lfh + m_prev
        m_t = jnp.maximum(inter, li)
        w_inter = jnp.exp(inter - m_t)
        wqk = jnp.exp(li - m_t) * jnp.sum(q * k, axis=1, keepdims=True)
        v_col = to_col(v)
        num = w_inter * jnp.sum(c_prev * q, axis=1, keepdims=True) + wqk * v_col
        den = w_inter * jnp.sum(n_prev * q, axis=1, keepdims=True) + wqk
        hh = num / jnp.maximum(jnp.abs(den), jnp.exp(-m_t))
        m_new = m_t
        a = jnp.exp(inter - m_new)
        wg = jnp.exp(li - m_new)
        c_out[0, h] = a * c_prev + (wg * v_col) * k
        n_out[0, h:h + 1, :] = a * n_prev + wg * k
        m_out[0, :, h:h + 1] = m_new
        hr = to_row(hh)
        hn = _rms(hr, nw_ref[:, h * ML_DH:(h + 1) * ML_DH])
        h_ref[0, :, h * ML_DH:(h + 1) * ML_DH] = jax.nn.sigmoid(o) * hn


def mlstm_step(ml, gates, gate_bias, norm_w, c0, n0, m0):
    b = ml.shape[0]
    w = ML_HEADS * ML_DH
    row = lambda n: pl.BlockSpec((1, 1, n), lambda i: (i, 0, 0))
    h, c, n, m = pl.pallas_call(
        _mlstm_step_kernel,
        grid=(b,),
        in_specs=[row(4 * w), row(LANES),
                  pl.BlockSpec((1, LANES), lambda i: (0, 0)),
                  pl.BlockSpec((1, w), lambda i: (0, 0)),
                  pl.BlockSpec((1, ML_HEADS, ML_DH, ML_DH), lambda i: (i, 0, 0, 0)),
                  pl.BlockSpec((1, ML_HEADS, ML_DH), lambda i: (i, 0, 0)),
                  row(ML_HEADS)],
        out_specs=[row(w),
                   pl.BlockSpec((1, ML_HEADS, ML_DH, ML_DH), lambda i: (i, 0, 0, 0)),
                   pl.BlockSpec((1, ML_HEADS, ML_DH), lambda i: (i, 0, 0)),
                   row(ML_HEADS)],
        out_shape=[jax.ShapeDtypeStruct((b, 1, w), F32),
                   jax.ShapeDtypeStruct((b, ML_HEADS, ML_DH, ML_DH), F32),
                   jax.ShapeDtypeStruct((b, ML_HEADS, ML_DH), F32),
                   jax.ShapeDtypeStruct((b, 1, ML_HEADS), F32)],
        compiler_params=_params(("arbitrary",)),
        name="mlstm_step",
    )(ml.reshape(b, 1, 4 * w), gates.reshape(b, 1, LANES), gate_bias, norm_w.reshape(1, w),
      c0, n0, m0.reshape(b, 1, ML_HEADS))
    return h.reshape(b, w), c, n, m.reshape(b, ML_HEADS)


def _log_sig_pair(z):
    lg = jnp.log(1.0 + jnp.exp(-jnp.abs(z)))
    lb = jnp.minimum(z, 0.0) - lg
    return lb, lb - z


def _suffix_sum(l1m, tri_bf16):
    hi = l1m.astype(BF16)
    lo = (l1m - hi.astype(F32)).astype(BF16)
    return (jnp.dot(hi, tri_bf16, preferred_element_type=F32) +
            jnp.dot(lo, tri_bf16, preferred_element_type=F32))


def _sb_prompt_kernel(q_ref, k_ref, v_ref, tri_ref, o_ref, *, bq):
    qi = pl.program_id(2)
    lane = lax.broadcasted_iota(I32, (bq, LANES), 1)
    q = q_ref[...] * (SB_DH ** -0.5)
    qh = [jnp.where((lane // SB_DH) == h, q, 0.0).astype(BF16) for h in range(2)]
    tri = tri_ref[...]
    row = lax.broadcasted_iota(I32, (bq, bq), 0)
    colm = lax.broadcasted_iota(I32, (bq, bq), 1)
    causal = colm < row

    def tile(kb, carry, masked):
        accs, sums = carry
        start = pl.multiple_of(kb * bq, bq)
        kblk = k_ref[pl.ds(start, bq), :].astype(BF16)
        vblk = v_ref[pl.ds(start, bq), :].astype(BF16)
        new_accs, new_sums = [], []
        for h in range(2):
            z = lax.dot_general(qh[h], kblk, (((1,), (1,)), ((), ())), preferred_element_type=F32)
            lb, l1m = _log_sig_pair(z)
            if masked:
                l1m = jnp.where(causal, l1m, 0.0)
            rev = _suffix_sum(l1m, tri)
            a = jnp.exp(lb + rev + sums[h])
            if masked:
                a = jnp.where(causal, a, 0.0)
            new_accs.append(accs[h] + jnp.dot(a.astype(BF16), vblk, preferred_element_type=F32))
            new_sums.append(sums[h] + rev[:, 0:1] + l1m[:, 0:1])
        return tuple(new_accs), tuple(new_sums)

    zero_acc = jnp.zeros((bq, LANES), F32)
    zero_sum = jnp.zeros((bq, 1), F32)
    carry = tile(qi, ((zero_acc, zero_acc), (zero_sum, zero_sum)), True)
    carry = lax.fori_loop(0, qi, lambda i, c: tile(qi - 1 - i, c, False), carry)
    accs, _ = carry
    o_ref[...] = jnp.where(lane < SB_DH, accs[0], accs[1])


def sb_prompt(sq, sk, sv, batch, seq, bq=256):
    nq = seq // bq
    pairs = SB_HEADS // 2
    ri = lax.broadcasted_iota(I32, (bq, bq), 0)
    ci = lax.broadcasted_iota(I32, (bq, bq), 1)
    tri = (ri > ci).astype(BF16)
    return pl.pallas_call(
        functools.partial(_sb_prompt_kernel, bq=bq),
        grid=(batch, pairs, nq),
        in_specs=[pl.BlockSpec((bq, LANES), lambda b, p, i: (b * nq + i, p)),
                  pl.BlockSpec((seq, LANES), lambda b, p, i: (b, p)),
                  pl.BlockSpec((seq, LANES), lambda b, p, i: (b, p)),
                  pl.BlockSpec((bq, bq), lambda b, p, i: (0, 0))],
        out_specs=pl.BlockSpec((bq, LANES), lambda b, p, i: (b * nq + i, p)),
        out_shape=jax.ShapeDtypeStruct((batch * seq, SB_HEADS * SB_DH), F32),
        compiler_params=_params(("arbitrary", "arbitrary", "arbitrary")),
        name="sb_prompt",
    )(sq, sk, sv, tri)


def _sb_decode_kernel(pt_ref, q_ref, *refs, pp):
    k_refs = refs[:pp]
    v_refs = refs[pp:2 * pp]
    tri_ref = refs[2 * pp]
    o_ref = refs[2 * pp + 1]
    acc_sc, sum_sc = refs[2 * pp + 2:]
    p = pl.program_id(1)
    w = SB_HEADS * SB_DH

    @pl.when(p == 0)
    def _():
        acc_sc[...] = jnp.zeros_like(acc_sc)
        sum_sc[...] = jnp.zeros_like(sum_sc)

    sub = lax.broadcasted_iota(I32, (SB_HEADS, w), 0)
    lane = lax.broadcasted_iota(I32, (SB_HEADS, w), 1)
    own = (lane // SB_DH) == sub
    qmat = jnp.where(own, q_ref[0] * (SB_DH ** -0.5), 0.0).astype(BF16)
    tri = tri_ref[...]
    acc = acc_sc[...]
    tot = sum_sc[...]
    for i in range(pp):
        kp = k_refs[i][0].astype(BF16)
        vp = v_refs[i][0].astype(BF16)
        z = lax.dot_general(qmat, kp, (((1,), (1,)), ((), ())), preferred_element_type=F32)
        lb, l1m = _log_sig_pair(z)
        rev = _suffix_sum(l1m, tri)
        a = jnp.exp(lb + rev + tot)
        acc = acc + jnp.dot(a.astype(BF16), vp, preferred_element_type=F32)
        tot = tot + rev[:, 0:1] + l1m[:, 0:1]
    acc_sc[...] = acc
    sum_sc[...] = tot

    @pl.when(p == pl.num_programs(1) - 1)
    def _():
        o_ref[0] = jnp.sum(jnp.where(own, acc, 0.0), axis=0, keepdims=True)


def sb_decode(sq, cache_k, cache_v, page_table, pp=4):
    b = sq.shape[0]
    n_pages = page_table.shape[1]
    w = SB_HEADS * SB_DH
    n_pool = cache_k.shape[0]
    ck = cache_k.reshape(n_pool, PAGE, w)
    cv = cache_v.reshape(n_pool, PAGE, w)
    ri = lax.broadcasted_iota(I32, (PAGE, PAGE), 0)
    ci = lax.broadcasted_iota(I32, (PAGE, PAGE), 1)
    tri = (ri > ci).astype(BF16)
    steps = n_pages // pp

    def page_spec(i):
        return pl.BlockSpec((1, PAGE, w),
                            lambda bi, p, pt: (pt[bi * n_pages + (n_pages - 1 - (p * pp + i))], 0, 0))

    grid_spec = pltpu.PrefetchScalarGridSpec(
        num_scalar_prefetch=1,
        grid=(b, steps),
        in_specs=[pl.BlockSpec((1, 1, w), lambda bi, p, pt: (bi, 0, 0))]
                 + [page_spec(i) for i in range(pp)] + [page_spec(i) for i in range(pp)]
                 + [pl.BlockSpec((PAGE, PAGE), lambda bi, p, pt: (0, 0))],
        out_specs=pl.BlockSpec((1, 1, w), lambda bi, p, pt: (bi, 0, 0)),
        scratch_shapes=[pltpu.VMEM((SB_HEADS, w), F32), pltpu.VMEM((SB_HEADS, 1), F32)],
    )
    out = pl.pallas_call(
        functools.partial(_sb_decode_kernel, pp=pp),
        grid_spec=grid_spec,
        out_shape=jax.ShapeDtypeStruct((b, 1, w), F32),
        compiler_params=_params(("arbitrary", "arbitrary")),
        name="sb_decode",
    )(page_table.reshape(-1), sq.reshape(b, 1, w), *([ck] * pp), *([cv] * pp), tri)
    return out.reshape(b, w)


def _lambda_value(lam_ref):
    lp = lam_ref[...]
    s1 = jnp.sum(lp[0:1, :] * lp[1:2, :], axis=1, keepdims=True)
    s2 = jnp.sum(lp[2:3, :] * lp[3:4, :], axis=1, keepdims=True)
    return jnp.exp(s1) - jnp.exp(s2) + DF_LAMBDA_INIT


def _diff_prompt_kernel(q_ref, k_ref, v_ref, lam_ref, nw_ref, o_ref, *, bq):
    qi = pl.program_id(2)
    lane = lax.broadcasted_iota(I32, (bq, LANES), 1)
    q = q_ref[...] * (DF_D ** -0.5)
    qc = [jnp.where((lane // DF_D) == c, q, 0.0).astype(BF16) for c in range(2)]
    row = lax.broadcasted_iota(I32, (bq, bq), 0)
    colm = lax.broadcasted_iota(I32, (bq, bq), 1)
    causal = colm <= row

    def tile(kb, carry, masked):
        start = pl.multiple_of(kb * bq, bq)
        kblk = k_ref[pl.ds(start, bq), :].astype(BF16)
        vblk = v_ref[pl.ds(start, bq), :].astype(BF16)
        out = []
        for c in range(2):
            m_prev, l_prev, acc = carry[c]
            z = lax.dot_general(qc[c], kblk, (((1,), (1,)), ((), ())), preferred_element_type=F32)
            if masked:
                z = jnp.where(causal, z, NEG_INF)
            m_new = jnp.maximum(m_prev, jnp.max(z, axis=1, keepdims=True))
            p = jnp.exp(z - m_new)
            alpha = jnp.exp(m_prev - m_new)
            l_new = alpha * l_prev + jnp.sum(p, axis=1, keepdims=True)
            acc = alpha * acc + jnp.dot(p.astype(BF16), vblk, preferred_element_type=F32)
            out.append((m_new, l_new, acc))
        return tuple(out)

    init = (jnp.full((bq, 1), NEG_INF, F32), jnp.zeros((bq, 1), F32), jnp.zeros((bq, LANES), F32))
    carry = tile(qi, (init, init), True)
    carry = lax.fori_loop(0, qi, lambda i, c: tile(qi - 1 - i, c, False), carry)
    lam = _lambda_value(lam_ref)
    o = carry[0][2] / carry[0][1] - lam * (carry[1][2] / carry[1][1])
    o_ref[...] = _rms(o, nw_ref[...]) * (1.0 - DF_LAMBDA_INIT)


def diff_prompt(q, k, v, lam_pack, norm_w, batch, seq, bq=256):
    nq = seq // bq
    return pl.pallas_call(
        functools.partial(_diff_prompt_kernel, bq=bq),
        grid=(batch, DF_HEADS, nq),
        in_specs=[pl.BlockSpec((bq, LANES), lambda b, h, i: (b * nq + i, h)),
                  pl.BlockSpec((seq, LANES), lambda b, h, i: (b, h)),
                  pl.BlockSpec((seq, LANES), lambda b, h, i: (b, h)),
                  pl.BlockSpec((8, DF_D), lambda b, h, i: (0, 0)),
                  pl.BlockSpec((1, DF_V), lambda b, h, i: (0, 0))],
        out_specs=pl.BlockSpec((bq, LANES), lambda b, h, i: (b * nq + i, h)),
        out_shape=jax.ShapeDtypeStruct((batch * seq, DF_HEADS * DF_V), F32),
        compiler_params=_params(("arbitrary", "arbitrary", "arbitrary")),
        name="diff_prompt",
    )(q, k, v, lam_pack, norm_w.reshape(1, DF_V))


def _diff_decode_kernel(pt_ref, q_ref, kn_ref, vn_ref, lam_ref, nw_ref, *refs, pp):
    k_refs = refs[:pp]
    v_refs = refs[pp:2 * pp]
    o_ref = refs[2 * pp]
    m_sc, l_sc, acc_sc = refs[2 * pp + 1:]
    p = pl.program_id(1)
    w = DF_HEADS * DF_V
    rows = 2 * DF_HEADS
    sub = lax.broadcasted_iota(I32, (rows, w), 0)
    lane = lax.broadcasted_iota(I32, (rows, w), 1)
    qmat = jnp.where((lane // DF_D) == sub, q_ref[0] * (DF_D ** -0.5), 0.0)

    @pl.when(p == 0)
    def _():
        m_sc[...] = jnp.sum(qmat * kn_ref[0], axis=1, keepdims=True)
        l_sc[...] = jnp.ones_like(l_sc)
        acc_sc[...] = jnp.broadcast_to(vn_ref[0], (rows, w))

    qb = qmat.astype(BF16)
    m_prev, l_prev, acc = m_sc[...], l_sc[...], acc_sc[...]
    for i in range(pp):
        kp = k_refs[i][0].astype(BF16)
        vp = v_refs[i][0].astype(BF16)
        z = lax.dot_general(qb, kp, (((1,), (1,)), ((), ())), preferred_element_type=F32)
        m_new = jnp.maximum(m_prev, jnp.max(z, axis=1, keepdims=True))
        pr = jnp.exp(z - m_new)
        alpha = jnp.exp(m_prev - m_new)
        l_prev = alpha * l_prev + jnp.sum(pr, axis=1, keepdims=True)
        acc = alpha * acc + jnp.dot(pr.astype(BF16), vp, preferred_element_type=F32)
        m_prev = m_new
    m_sc[...] = m_prev
    l_sc[...] = l_prev
    acc_sc[...] = acc

    @pl.when(p == pl.num_programs(1) - 1)
    def _():
        lam = _lambda_value(lam_ref)
        coef = jnp.where((sub % 2) == 0, 1.0, -lam)
        r = jnp.where((lane // DF_V) == (sub // 2), acc / l_prev * coef, 0.0)
        o = jnp.sum(r, axis=0, keepdims=True)
        for h in range(DF_HEADS):
            sl = slice(h * DF_V, (h + 1) * DF_V)
            o_ref[0, :, sl] = _rms(o[:, sl], nw_ref[...]) * (1.0 - DF_LAMBDA_INIT)


def diff_decode(q, k_new, v_new, cache_k, cache_v, page_table, lam_pack, norm_w, pp=4):
    b = q.shape[0]
    n_pages = page_table.shape[1]
    w = DF_HEADS * DF_V
    n_pool = cache_k.shape[0]
    ck = cache_k.reshape(n_pool, PAGE, w)
    cv = cache_v.reshape(n_pool, PAGE, w)
    steps = n_pages // pp
    rows = 2 * DF_HEADS

    def page_spec(i):
        return pl.BlockSpec((1, PAGE, w), lambda bi, p, pt: (pt[bi * n_pages + p * pp + i], 0, 0))

    row = pl.BlockSpec((1, 1, w), lambda bi, p, pt: (bi, 0, 0))
    grid_spec = pltpu.PrefetchScalarGridSpec(
        num_scalar_prefetch=1,
        grid=(b, steps),
        in_specs=[row, row, row,
                  pl.BlockSpec((8, DF_D), lambda bi, p, pt: (0, 0)),
                  pl.BlockSpec((1, DF_V), lambda bi, p, pt: (0, 0))]
                 + [page_spec(i) for i in range(pp)] + [page_spec(i) for i in range(pp)],
        out_specs=row,
        scratch_shapes=[pltpu.VMEM((rows, 1), F32), pltpu.VMEM((rows, 1), F32), pltpu.VMEM((rows, w), F32)],
    )
    out = pl.pallas_call(
        functools.partial(_diff_decode_kernel, pp=pp),
        grid_spec=grid_spec,
        out_shape=jax.ShapeDtypeStruct((b, 1, w), F32),
        compiler_params=_params(("arbitrary", "arbitrary")),
        name="diff_decode",
    )(page_table.reshape(-1), q.reshape(b, 1, w), k_new.reshape(b, 1, w), v_new.reshape(b, 1, w),
      lam_pack, norm_w.reshape(1, DF_V), *([ck] * pp), *([cv] * pp))
    return out.reshape(b, w)


def _top_rows(tiles, k):
    r, n = tiles[0].shape
    nt = len(tiles)
    big = float(nt * r)
    ridx = [lax.broadcasted_iota(I32, (r, n), 0).astype(F32) + float(j * r) for j in range(nt)]
    kio = lax.broadcasted_iota(I32, (k, n), 0)

    def body(i, carry):
        cur, vals, idxs = carry
        m = cur[0]
        for t in cur[1:]:
            m = jnp.maximum(m, t)
        m = jnp.max(m, axis=0, keepdims=True)
        cand = [jnp.where(t == m, ix, big) for t, ix in zip(cur, ridx)]
        f = cand[0]
        for c in cand[1:]:
            f = jnp.minimum(f, c)
        f = jnp.min(f, axis=0, keepdims=True)
        cur = tuple(jnp.where(ix == f, NEG_INF, t) for t, ix in zip(cur, ridx))
        vals = jnp.where(kio == i, m, vals)
        idxs = jnp.where(kio == i, f, idxs)
        return cur, vals, idxs

    init = (tuple(tiles), jnp.zeros((k, n), F32), jnp.zeros((k, n), F32))
    _, vals, idxs = lax.fori_loop(0, k, body, init)
    return vals, idxs


def _pick_rows(table, sel):
    kk = table.shape[0]
    out = jnp.zeros(sel.shape, F32)
    for a in range(kk):
        out = jnp.where(sel == float(a), table[a:a + 1, :], out)
    return out


def _peer_select_kernel(h_ref, nw_ref, wq_ref, keys_ref, xn_ref, e_ref, g_ref):
    kk = PEER_TOPK
    xn = _rms(h_ref[...], nw_ref[...])
    xn_ref[...] = xn
    q = jnp.dot(xn.astype(BF16), wq_ref[...], preferred_element_type=F32)
    for h in range(PEER_HEADS):
        tops = []
        for c in range(2):
            j = h * 2 + c
            s_t = _bdot_nt(keys_ref[c], q[:, j * LANES:(j + 1) * LANES])
            tops.append(_top_rows([s_t], kk))
        (s1, i1), (s2, i2) = tops
        cand = [s1[a:a + 1, :] + s2 for a in range(kk)]
        c_top, c_idx = _top_rows(cand, kk)
        ia = jnp.floor(c_idx * (1.0 / kk))
        ib = c_idx - ia * kk
        e = _pick_rows(i1, ia) * float(PEER_NKEYS) + _pick_rows(i2, ib)
        ex = jnp.exp(c_top - jnp.max(c_top, axis=0, keepdims=True))
        g = ex / jnp.sum(ex, axis=0, keepdims=True)
        e_ref[0, h * kk:(h + 1) * kk, :] = e.astype(I32)
        g_ref[0, h * kk:(h + 1) * kk, :] = g


def peer_select(h, nw, wq_bf16, keys):
    t, d = h.shape
    tb = LANES
    nb = t // tb
    picks = PEER_HEADS * PEER_TOPK
    return pl.pallas_call(
        _peer_select_kernel,
        grid=(nb,),
        in_specs=[pl.BlockSpec((tb, d), lambda i: (i, 0)),
                  pl.BlockSpec((1, d), lambda i: (0, 0)),
                  pl.BlockSpec(wq_bf16.shape, lambda i: (0, 0)),
                  pl.BlockSpec(keys.shape, lambda i: (0, 0, 0))],
        out_specs=[pl.BlockSpec((tb, d), lambda i: (i, 0)),
                   pl.BlockSpec((1, picks, tb), lambda i: (i, 0, 0)),
                   pl.BlockSpec((1, picks, tb), lambda i: (i, 0, 0))],
        out_shape=[jax.ShapeDtypeStruct((t, d), F32),
                   jax.ShapeDtypeStruct((nb, picks, tb), I32),
                   jax.ShapeDtypeStruct((nb, picks, tb), F32)],
        compiler_params=_params(("arbitrary",)),
        name="peer_select",
    )(h, nw.reshape(1, d), wq_bf16, keys)


def _peer_gather_kernel(e_hbm, uv_hbm, xn_ref, g_ref, h_ref, o_ref, ids_smem, rows, ids_sem, row_sem,
                        *, n_tok, n_slots):
    i = pl.program_id(0)
    picks = PEER_HEADS * PEER_TOPK
    d = xn_ref.shape[1]

    ids_copy = pltpu.make_async_copy(e_hbm.at[i], ids_smem, ids_sem)
    ids_copy.start()
    ids_copy.wait()

    def issue(t):
        slot = t % n_slots

        def one(j, _):
            e = ids_smem[j, t]
            pltpu.make_async_copy(uv_hbm.at[pl.ds(e, 1)], rows.at[slot, pl.ds(j, 1)], row_sem.at[slot]).start()
            return 0

        lax.fori_loop(0, picks, one, 0, unroll=8)

    def wait(t):
        slot = t % n_slots
        pltpu.make_async_copy(uv_hbm.at[pl.ds(0, picks)], rows.at[slot], row_sem.at[slot]).wait()

    lane = lax.broadcasted_iota(I32, (picks, LANES), 1)
    o_ref[...] = h_ref[...]
    issue(0)

    def step(t, _):
        @pl.when(t + 1 < n_tok)
        def _():
            issue(t + 1)

        wait(t)
        slot = t % n_slots
        x = xn_ref[pl.ds(t, 1), :]
        act = jnp.sum(rows[slot, :, 0:d] * x, axis=1, keepdims=True)
        g = jnp.sum(jnp.where(lane == t, g_ref[0], 0.0), axis=1, keepdims=True)
        wgt = g * jax.nn.gelu(act)
        mix = jnp.sum(wgt * rows[slot, :, d:2 * d], axis=0, keepdims=True)
        o_ref[pl.ds(t, 1), :] = h_ref[pl.ds(t, 1), :] + mix
        return 0

    lax.fori_loop(0, n_tok, step, 0)


def peer_gather(e_blocks, g_blocks, uv, xn, h, n_tok):
    t, d = h.shape
    tb = LANES
    nb = t // tb
    picks = PEER_HEADS * PEER_TOPK
    n_slots = 2
    return pl.pallas_call(
        functools.partial(_peer_gather_kernel, n_tok=n_tok, n_slots=n_slots),
        grid=(nb,),
        in_specs=[pl.BlockSpec(memory_space=pl.ANY),
                  pl.BlockSpec(memory_space=pl.ANY),
                  pl.BlockSpec((tb, d), lambda i: (i, 0)),
                  pl.BlockSpec((1, picks, tb), lambda i: (i, 0, 0)),
                  pl.BlockSpec((tb, d), lambda i: (i, 0))],
        out_specs=pl.BlockSpec((tb, d), lambda i: (i, 0)),
        out_shape=jax.ShapeDtypeStruct((t, d), F32),
        scratch_shapes=[pltpu.SMEM((picks, tb), I32),
                        pltpu.VMEM((n_slots, picks, 2 * d), F32),
                        pltpu.SemaphoreType.DMA,
                        pltpu.SemaphoreType.DMA((n_slots,))],
        compiler_params=_params(("arbitrary",)),
        name="peer_gather",
    )(e_blocks, uv, xn, g_blocks, h)


def peer_layer(h, nw, wq_bf16, keys, uv):
    t = h.shape[0]
    tp = -(-t // LANES) * LANES
    hp = h if tp == t else jnp.pad(h, ((0, tp - t), (0, 0)))
    xn, e_blocks, g_blocks = peer_select(hp, nw, wq_bf16, keys)
    out = peer_gather(e_blocks, g_blocks, uv, xn, hp, min(t, LANES))
    return out if tp == t else out[:t]


def kernel(x_prompt, x_sample, p_prompt, p_sample, state_mlstm_C, state_mlstm_n, state_mlstm_m, cache_sb_k, cache_sb_v, cache_diff_k, cache_diff_v, page_table, norm_mix0, w_in0, b_igate0, b_fgate0, mlstm_norm0, w_out0, norm_ffn0, peer_wq0, peer_keys0, peer_u0, peer_v0, ple_gate0, ple_proj0, norm_mix1, w_qkv1, lambda_q1, lambda_k1, lambda_q2, lambda_k2, diff_norm1, w_out1, norm_ffn1, peer_wq1, peer_keys1, peer_u1, peer_v1, ple_gate1, ple_proj1, final_norm):
    bp, sp, d = x_prompt.shape
    bs, ss, _ = x_sample.shape
    assert ss == 1
    ml_w = ML_HEADS * ML_DH
    sb_w = SB_HEADS * SB_DH
    df_w = DF_HEADS * DF_V

    g0 = 4 * ml_w
    g1 = g0 + 2 * ML_HEADS
    gates_w = jnp.pad(w_in0[:, g0:g1], ((0, 0), (0, LANES - 2 * ML_HEADS)))
    w_in = jnp.concatenate([w_in0[:, :g0], w_in0[:, g1:], gates_w], axis=1).astype(BF16)
    in_splits = (4 * ml_w, sb_w, sb_w, sb_w, LANES)
    gate_bias = jnp.pad(jnp.concatenate([b_igate0, b_fgate0]), (0, LANES - 2 * ML_HEADS)).reshape(1, LANES)
    w_out0_b = w_out0.astype(BF16)
    w_qkv = w_qkv1.astype(BF16)
    w_out1_b = w_out1.astype(BF16)
    lam_pack = jnp.pad(jnp.stack([lambda_q1, lambda_k1, lambda_q2, lambda_k2]), ((0, 4), (0, 0)))
    uv0 = jnp.concatenate([peer_u0, peer_v0], axis=1)
    uv1 = jnp.concatenate([peer_u1, peer_v1], axis=1)
    peer0 = (norm_ffn0, peer_wq0.astype(BF16), peer_keys0, uv0)
    peer1 = (norm_ffn1, peer_wq1.astype(BF16), peer_keys1, uv1)
    ple0 = (ple_gate0.astype(BF16), ple_proj0.astype(BF16))
    ple1 = (ple_gate1.astype(BF16), ple_proj1.astype(BF16))

    def layer0_tail(h, mix_acts, p0):
        h = proj_residual(h, mix_acts, [w_out0_b[:ml_w], w_out0_b[ml_w:]])
        h = peer_layer(h, *peer0)
        return ple(h, p0, *ple0, final_norm, False)

    def layer1_tail(h, o, p1):
        h = proj_residual(h, [o], [w_out1_b])
        h = peer_layer(h, *peer1)
        return ple(h, p1, *ple1, final_norm, True)

    tp = bp * sp
    h = x_prompt.reshape(tp, d)
    ml, sq, sk, sv, gates = norm_proj(h, norm_mix0, w_in, in_splits)
    h_ml, c_p, n_p, m_p = mlstm_prompt(ml, gates, gate_bias, mlstm_norm0, bp, sp)
    h_sb = sb_prompt(sq, sk, sv, bp, sp)
    h = layer0_tail(h, [h_ml, h_sb], p_prompt[0].reshape(tp, -1))
    q1, k1, v1 = norm_proj(h, norm_mix1, w_qkv, (df_w, df_w, df_w))
    o = diff_prompt(q1, k1, v1, lam_pack, diff_norm1, bp, sp)
    y_prompt = layer1_tail(h, o, p_prompt[1].reshape(tp, -1)).reshape(bp, sp, d)

    hs = x_sample.reshape(bs, d)
    ml_s, sq_s, sk_s, sv_s, gates_s = norm_proj(hs, norm_mix0, w_in, in_splits)
    h_ml_s, c_s, n_s, m_s = mlstm_step(ml_s, gates_s, gate_bias, mlstm_norm0,
                                       state_mlstm_C, state_mlstm_n, state_mlstm_m)
    h_sb_s = sb_decode(sq_s, cache_sb_k, cache_sb_v, page_table)
    hs = layer0_tail(hs, [h_ml_s, h_sb_s], p_sample[0].reshape(bs, -1))
    q1s, k1s, v1s = norm_proj(hs, norm_mix1, w_qkv, (df_w, df_w, df_w))
    o_s = diff_decode(q1s, k1s, v1s, cache_diff_k, cache_diff_v, page_table, lam_pack, diff_norm1)
    y_sample = layer1_tail(hs, o_s, p_sample[1].reshape(bs, -1)).reshape(bs, ss, d)

    return (y_prompt, y_sample,
            c_p, n_p, m_p.reshape(bp, ML_HEADS),
            sk.reshape(bp, sp, SB_HEADS, SB_DH), sv.reshape(bp, sp, SB_HEADS, SB_DH),
            k1.reshape(bp, sp, DF_HEADS, DF_V), v1.reshape(bp, sp, DF_HEADS, DF_V),
            c_s, n_s, m_s,
            sk_s.reshape(bs, ss, SB_HEADS, SB_DH), sv_s.reshape(bs, ss, SB_HEADS, SB_DH),
            k1s.reshape(bs, ss, DF_HEADS, DF_V), v1s.reshape(bs, ss, DF_HEADS, DF_V))
```

```python
import functools
import math

import jax
import jax.numpy as jnp
from jax import lax
from jax.experimental import pallas as pl
from jax.experimental.pallas import tpu as pltpu

F32 = jnp.float32
BF16 = jnp.bfloat16
I32 = jnp.int32

RMS_EPS = 1e-6
LANES = 128
ML_HEADS, ML_DH, ML_CHUNK = 4, 128, 64
SB_HEADS, SB_DH = 8, 64
DF_HEADS, DF_D, DF_V = 8, 64, 128
DF_LAMBDA_INIT = 0.8 - 0.6 * math.exp(-0.3 * 1)
PEER_HEADS, PEER_NKEYS, PEER_TOPK = 8, 128, 16
PAGE = 128
VMEM_LIMIT = 56 * 1024 * 1024
NEG_INF = float("-inf")


def _params(sem):
    return pltpu.CompilerParams(dimension_semantics=sem, vmem_limit_bytes=VMEM_LIMIT)


def _bdot(a, b):
    return jnp.dot(a.astype(BF16), b.astype(BF16), preferred_element_type=F32)


def _bdot_nt(a, b):
    return lax.dot_general(a.astype(BF16), b.astype(BF16), (((1,), (1,)), ((), ())),
                           preferred_element_type=F32)


def _rms(x, w):
    return x * lax.rsqrt(jnp.mean(x * x, axis=-1, keepdims=True) + RMS_EPS) * w


def _row_tile(t):
    for tm in (256, 128, 64, 32, 16, 8):
        if t % tm == 0:
            return tm
    raise ValueError(f"row count {t} must be a multiple of 8")


def _norm_proj_kernel(x_ref, nw_ref, w_ref, *o_refs, splits):
    xn = _rms(x_ref[...], nw_ref[...]).astype(BF16)
    off = 0
    for o_ref, n in zip(o_refs, splits):
        o_ref[...] = jnp.dot(xn, w_ref[:, off:off + n], preferred_element_type=F32)
        off += n


def norm_proj(x, nw, w_bf16, splits):
    t, d = x.shape
    n = w_bf16.shape[1]
    assert sum(splits) == n and all(s % LANES == 0 for s in splits)
    tm = _row_tile(t)
    return pl.pallas_call(
        functools.partial(_norm_proj_kernel, splits=tuple(splits)),
        grid=(t // tm,),
        in_specs=[pl.BlockSpec((tm, d), lambda i: (i, 0)),
                  pl.BlockSpec((1, d), lambda i: (0, 0)),
                  pl.BlockSpec((d, n), lambda i: (0, 0))],
        out_specs=[pl.BlockSpec((tm, s), lambda i: (i, 0)) for s in splits],
        out_shape=[jax.ShapeDtypeStruct((t, s), F32) for s in splits],
        compiler_params=_params(("arbitrary",)),
        name="norm_proj",
    )(x, nw.reshape(1, d), w_bf16)


def _proj_res_kernel(*refs, n_in):
    res_ref = refs[0]
    a_refs = refs[1:1 + n_in]
    w_refs = refs[1 + n_in:1 + 2 * n_in]
    o_ref = refs[1 + 2 * n_in]
    acc = res_ref[...]
    for a_ref, w_ref in zip(a_refs, w_refs):
        acc = acc + jnp.dot(a_ref[...].astype(BF16), w_ref[...], preferred_element_type=F32)
    o_ref[...] = acc


def proj_residual(res, acts, ws_bf16):
    t, d = res.shape
    tm = _row_tile(t)
    n_in = len(acts)
    in_specs = [pl.BlockSpec((tm, d), lambda i: (i, 0))]
    in_specs += [pl.BlockSpec((tm, a.shape[1]), lambda i: (i, 0)) for a in acts]
    in_specs += [pl.BlockSpec(w.shape, lambda i: (0, 0)) for w in ws_bf16]
    return pl.pallas_call(
        functools.partial(_proj_res_kernel, n_in=n_in),
        grid=(t // tm,),
        in_specs=in_specs,
        out_specs=pl.BlockSpec((tm, d), lambda i: (i, 0)),
        out_shape=jax.ShapeDtypeStruct((t, d), F32),
        compiler_params=_params(("arbitrary",)),
        name="proj_residual",
    )(res, *acts, *ws_bf16)


def _ple_kernel(h_ref, p_ref, wg_ref, wp_ref, fn_ref, o_ref, *, final_norm):
    h = h_ref[...]
    gate = jax.nn.sigmoid(jnp.dot(h.astype(BF16), wg_ref[...], preferred_element_type=F32))
    emb = jnp.dot(p_ref[...].astype(BF16), wp_ref[...], preferred_element_type=F32)
    y = h + gate * emb
    if final_norm:
        y = _rms(y, fn_ref[...])
    o_ref[...] = y


def ple(h, p, wg_bf16, wp_bf16, fnorm, final_norm):
    t, d = h.shape
    pd = p.shape[1]
    tm = _row_tile(t)
    return pl.pallas_call(
        functools.partial(_ple_kernel, final_norm=final_norm),
        grid=(t // tm,),
        in_specs=[pl.BlockSpec((tm, d), lambda i: (i, 0)),
                  pl.BlockSpec((tm, pd), lambda i: (i, 0)),
                  pl.BlockSpec((d, d), lambda i: (0, 0)),
                  pl.BlockSpec((pd, d), lambda i: (0, 0)),
                  pl.BlockSpec((1, d), lambda i: (0, 0))],
        out_specs=pl.BlockSpec((tm, d), lambda i: (i, 0)),
        out_shape=jax.ShapeDtypeStruct((t, d), F32),
        compiler_params=_params(("arbitrary",)),
        name="ple",
    )(h, p, wg_bf16, wp_bf16, fnorm.reshape(1, d))


def _mlstm_chunk_kernel(q_ref, k_ref, v_ref, o_ref, g_ref, gb_ref, nw_ref,
                        h_ref, c_out, n_out, m_out, c_sc, n_sc, m_sc):
    ci = pl.program_id(1)
    L = ML_CHUNK

    @pl.when(ci == 0)
    def _():
        c_sc[...] = jnp.zeros_like(c_sc)
        n_sc[...] = jnp.zeros_like(n_sc)
        m_sc[...] = jnp.zeros_like(m_sc)

    hp = lax.Precision.HIGHEST
    g = g_ref[...] + gb_ref[...]
    lf = jax.nn.log_sigmoid(g)
    ri = lax.broadcasted_iota(I32, (L, L), 0)
    cj = lax.broadcasted_iota(I32, (L, L), 1)
    tril = (cj <= ri)
    low = tril.astype(F32)
    upp = (ri <= cj).astype(F32)
    eye = (lax.broadcasted_iota(I32, (LANES, LANES), 0) ==
           lax.broadcasted_iota(I32, (LANES, LANES), 1)).astype(F32)
    nt = (((1,), (1,)), ((), ()))
    g_t = lax.dot_general(eye, g, nt, precision=hp, preferred_element_type=F32)
    lf_t = lax.dot_general(eye, lf, nt, precision=hp, preferred_element_type=F32)
    b_c = jnp.dot(low, lf, precision=hp, preferred_element_type=F32)
    b_r = jnp.dot(lf_t, upp, precision=hp, preferred_element_type=F32)

    for h in range(ML_HEADS):
        sl = slice(h * ML_DH, (h + 1) * ML_DH)
        q = q_ref[:, sl] * (ML_DH ** -0.5)
        k = k_ref[:, sl]
        v = v_ref[:, sl]
        li_c = g[:, h:h + 1]
        li_r = g_t[h:h + 1, :]
        bc = b_c[:, ML_HEADS + h:ML_HEADS + h + 1]
        br = b_r[ML_HEADS + h:ML_HEADS + h + 1, :]
        m_prev = m_sc[:, h:h + 1]
        c_prev = c_sc[h]
        n_prev = n_sc[h:h + 1, :]
        dmat = jnp.where(tril, bc - br + li_r, NEG_INF)
        inter = bc + m_prev
        m_t = jnp.maximum(inter, jnp.max(dmat, axis=1, keepdims=True))
        w_inter = jnp.exp(inter - m_t)
        wqk = jnp.exp(dmat - m_t) * _bdot_nt(q, k)
        num = w_inter * _bdot_nt(q, c_prev) + _bdot(wqk, v)
        den = w_inter * jnp.sum(q * n_prev, axis=1, keepdims=True) + jnp.sum(wqk, axis=1, keepdims=True)
        hh = num / jnp.maximum(jnp.abs(den), jnp.exp(-m_t))
        b_last = br[:, L - 1:L]
        g_r = b_last - br + li_r
        g_c = b_last - bc + li_c
        m_new = jnp.maximum(b_last + m_prev, jnp.max(g_r, axis=1, keepdims=True))
        a = jnp.exp(b_last + m_prev - m_new)
        wg_c = jnp.exp(g_c - m_new)
        wv = (wg_c * v).astype(BF16)
        c_new = a * c_prev + lax.dot_general(wv, k.astype(BF16), (((0,), (0,)), ((), ())),
                                             preferred_element_type=F32)
        n_new = a * n_prev + jnp.sum(wg_c * k, axis=0, keepdims=True)
        c_sc[h] = c_new
        n_sc[h:h + 1, :] = n_new
        m_sc[:, h:h + 1] = m_new
        hn = _rms(hh, nw_ref[:, sl])
        h_ref[:, sl] = jax.nn.sigmoid(o_ref[:, sl]) * hn

    @pl.when(ci == pl.num_programs(1) - 1)
    def _():
        c_out[0] = c_sc[...]
        n_out[0] = n_sc[...]
        m_out[0] = m_sc[:, 0:ML_HEADS]


def mlstm_prompt(ml, gates, gate_bias, norm_w, batch, seq):
    L = ML_CHUNK
    nc = seq // L
    w = ML_HEADS * ML_DH
    col = lambda j: pl.BlockSpec((L, w), lambda b, c: (b * nc + c, j))
    return pl.pallas_call(
        _mlstm_chunk_kernel,
        grid=(batch, nc),
        in_specs=[col(0), col(1), col(2), col(3),
                  pl.BlockSpec((L, LANES), lambda b, c: (b * nc + c, 0)),
                  pl.BlockSpec((1, LANES), lambda b, c: (0, 0)),
                  pl.BlockSpec((1, w), lambda b, c: (0, 0))],
        out_specs=[pl.BlockSpec((L, w), lambda b, c: (b * nc + c, 0)),
                   pl.BlockSpec((1, ML_HEADS, ML_DH, ML_DH), lambda b, c: (b, 0, 0, 0)),
                   pl.BlockSpec((1, ML_HEADS, ML_DH), lambda b, c: (b, 0, 0)),
                   pl.BlockSpec((1, 1, ML_HEADS), lambda b, c: (b, 0, 0))],
        out_shape=[jax.ShapeDtypeStruct((batch * seq, w), F32),
                   jax.ShapeDtypeStruct((batch, ML_HEADS, ML_DH, ML_DH), F32),
                   jax.ShapeDtypeStruct((batch, ML_HEADS, ML_DH), F32),
                   jax.ShapeDtypeStruct((batch, 1, ML_HEADS), F32)],
        scratch_shapes=[pltpu.VMEM((ML_HEADS, ML_DH, ML_DH), F32),
                        pltpu.VMEM((ML_HEADS, ML_DH), F32),
                        pltpu.VMEM((1, LANES), F32)],
        compiler_params=_params(("arbitrary", "arbitrary")),
        name="mlstm_chunk",
    )(ml, ml, ml, ml, gates, gate_bias, norm_w.reshape(1, w))


def _mlstm_step_kernel(x_ref, g_ref, gb_ref, nw_ref, c_ref, n_ref, m_ref,
                       h_ref, c_out, n_out, m_out):
    w = ML_HEADS * ML_DH
    g = g_ref[0] + gb_ref[...]
    lf = jax.nn.log_sigmoid(g)
    eye = (lax.broadcasted_iota(I32, (ML_DH, ML_DH), 0) ==
           lax.broadcasted_iota(I32, (ML_DH, ML_DH), 1)).astype(F32)

    def to_col(row):
        return jnp.sum(eye * row, axis=1, keepdims=True)

    def to_row(colv):
        return jnp.sum(eye * colv, axis=0, keepdims=True)

    x = x_ref[0]
    for h in range(ML_HEADS):
        q = x[:, h * ML_DH:(h + 1) * ML_DH] * (ML_DH ** -0.5)
        k = x[:, w + h * ML_DH:w + (h + 1) * ML_DH]
        v = x[:, 2 * w + h * ML_DH:2 * w + (h + 1) * ML_DH]
        o = x[:, 3 * w + h * ML_DH:3 * w + (h + 1) * ML_DH]
        li = g[:, h:h + 1]
        lfh = lf[:, ML_HEADS + h:ML_HEADS + h + 1]
        m_prev = m_ref[0][:, h:h + 1]
        c_prev = c_ref[0, h]
        n_prev = n_ref[0][h:h + 1, :]
        inter = lfh + m_prev
        m_t = jnp.maximum(inter, li)
        w_inter = jnp.exp(inter - m_t)
        wqk = jnp.exp(li - m_t) * jnp.sum(q * k, axis=1, keepdims=True)
        v_col = to_col(v)
        num = w_inter * jnp.sum(c_prev * q, axis=1, keepdims=True) + wqk * v_col
        den = w_inter * jnp.sum(n_prev * q, axis=1, keepdims=True) + wqk
        hh = num / jnp.maximum(jnp.abs(den), jnp.exp(-m_t))
        m_new = m_t
        a = jnp.exp(inter - m_new)
        wg = jnp.exp(li - m_new)
        c_out[0, h] = a * c_prev + (wg * v_col) * k
        n_out[0, h:h + 1, :] = a * n_prev + wg * k
        m_out[0, :, h:h + 1] = m_new
        hr = to_row(hh)
        hn = _rms(hr, nw_ref[:, h * ML_DH:(h + 1) * ML_DH])
        h_ref[0, :, h * ML_DH:(h + 1) * ML_DH] = jax.nn.sigmoid(o) * hn


def mlstm_step(ml, gates, gate_bias, norm_w, c0, n0, m0):
    b = ml.shape[0]
    w = ML_HEADS * ML_DH
    row = lambda n: pl.BlockSpec((1, 1, n), lambda i: (i, 0, 0))
    h, c, n, m = pl.pallas_call(
        _mlstm_step_kernel,
        grid=(b,),
        in_specs=[row(4 * w), row(LANES),
                  pl.BlockSpec((1, LANES), lambda i: (0, 0)),
                  pl.BlockSpec((1, w), lambda i: (0, 0)),
                  pl.BlockSpec((1, ML_HEADS, ML_DH, ML_DH), lambda i: (i, 0, 0, 0)),
                  pl.BlockSpec((1, ML_HEADS, ML_DH), lambda i: (i, 0, 0)),
                  row(ML_HEADS)],
        out_specs=[row(w),
                   pl.BlockSpec((1, ML_HEADS, ML_DH, ML_DH), lambda i: (i, 0, 0, 0)),
                   pl.BlockSpec((1, ML_HEADS, ML_DH), lambda i: (i, 0, 0)),
                   row(ML_HEADS)],
        out_shape=[jax.ShapeDtypeStruct((b, 1, w), F32),
                   jax.ShapeDtypeStruct((b, ML_HEADS, ML_DH, ML_DH), F32),
                   jax.ShapeDtypeStruct((b, ML_HEADS, ML_DH), F32),
                   jax.ShapeDtypeStruct((b, 1, ML_HEADS), F32)],
        compiler_params=_params(("arbitrary",)),
        name="mlstm_step",
    )(ml.reshape(b, 1, 4 * w), gates.reshape(b, 1, LANES), gate_bias, norm_w.reshape(1, w),
      c0, n0, m0.reshape(b, 1, ML_HEADS))
    return h.reshape(b, w), c, n, m.reshape(b, ML_HEADS)


def _log_sig_pair(z):
    lg = jnp.log(1.0 + jnp.exp(-jnp.abs(z)))
    lb = jnp.minimum(z, 0.0) - lg
    return lb, lb - z


def _suffix_sum(l1m, tri_bf16):
    hi = l1m.astype(BF16)
    lo = (l1m - hi.astype(F32)).astype(BF16)
    return (jnp.dot(hi, tri_bf16, preferred_element_type=F32) +
            jnp.dot(lo, tri_bf16, preferred_element_type=F32))


def _sb_prompt_kernel(q_ref, k_ref, v_ref, tri_ref, o_ref, *, bq):
    qi = pl.program_id(2)
    lane = lax.broadcasted_iota(I32, (bq, LANES), 1)
    q = q_ref[...] * (SB_DH ** -0.5)
    qh = [jnp.where((lane // SB_DH) == h, q, 0.0).astype(BF16) for h in range(2)]
    tri = tri_ref[...]
    row = lax.broadcasted_iota(I32, (bq, bq), 0)
    colm = lax.broadcasted_iota(I32, (bq, bq), 1)
    causal = colm < row

    def tile(kb, carry, masked):
        accs, sums = carry
        start = pl.multiple_of(kb * bq, bq)
        kblk = k_ref[pl.ds(start, bq), :].astype(BF16)
        vblk = v_ref[pl.ds(start, bq), :].astype(BF16)
        new_accs, new_sums = [], []
        for h in range(2):
            z = lax.dot_general(qh[h], kblk, (((1,), (1,)), ((), ())), preferred_element_type=F32)
            lb, l1m = _log_sig_pair(z)
            if masked:
                l1m = jnp.where(causal, l1m, 0.0)
            rev = _suffix_sum(l1m, tri)
            a = jnp.exp(lb + rev + sums[h])
            if masked:
                a = jnp.where(causal, a, 0.0)
            new_accs.append(accs[h] + jnp.dot(a.astype(BF16), vblk, preferred_element_type=F32))
            new_sums.append(sums[h] + rev[:, 0:1] + l1m[:, 0:1])
        return tuple(new_accs), tuple(new_sums)

    zero_acc = jnp.zeros((bq, LANES), F32)
    zero_sum = jnp.zeros((bq, 1), F32)
    carry = tile(qi, ((zero_acc, zero_acc), (zero_sum, zero_sum)), True)
    carry = lax.fori_loop(0, qi, lambda i, c: tile(qi - 1 - i, c, False), carry)
    accs, _ = carry
    o_ref[...] = jnp.where(lane < SB_DH, accs[0], accs[1])


def sb_prompt(sq, sk, sv, batch, seq, bq=256):
    nq = seq // bq
    pairs = SB_HEADS // 2
    ri = lax.broadcasted_iota(I32, (bq, bq), 0)
    ci = lax.broadcasted_iota(I32, (bq, bq), 1)
    tri = (ri > ci).astype(BF16)
    return pl.pallas_call(
        functools.partial(_sb_prompt_kernel, bq=bq),
        grid=(batch, pairs, nq),
        in_specs=[pl.BlockSpec((bq, LANES), lambda b, p, i: (b * nq + i, p)),
                  pl.BlockSpec((seq, LANES), lambda b, p, i: (b, p)),
                  pl.BlockSpec((seq, LANES), lambda b, p, i: (b, p)),
                  pl.BlockSpec((bq, bq), lambda b, p, i: (0, 0))],
        out_specs=pl.BlockSpec((bq, LANES), lambda b, p, i: (b * nq + i, p)),
        out_shape=jax.ShapeDtypeStruct((batch * seq, SB_HEADS * SB_DH), F32),
        compiler_params=_params(("arbitrary", "arbitrary", "arbitrary")),
        name="sb_prompt",
    )(sq, sk, sv, tri)


def _lane_group_sums(x, ind_bf16):
    n, s, l = x.shape
    x2 = x.reshape(n * s, l)
    hi = x2.astype(BF16)
    lo = (x2 - hi.astype(F32)).astype(BF16)
    y = (jnp.dot(hi, ind_bf16, preferred_element_type=F32) +
         jnp.dot(lo, ind_bf16, preferred_element_type=F32))
    return y.reshape(n, s, ind_bf16.shape[1])


def _sb_decode_kernel(pt_ref, q_ref, *refs, pp):
    k_refs = refs[:pp]
    v_refs = refs[pp:2 * pp]
    o_ref = refs[2 * pp]
    acc_sc, run_sc, lb_sc, l1m_sc = refs[2 * pp + 1:]
    p = pl.program_id(1)

    @pl.when(p == 0)
    def _():
        acc_sc[...] = jnp.zeros_like(acc_sc)
        run_sc[...] = jnp.zeros_like(run_sc)

    q = q_ref[0] * (SB_DH ** -0.5)
    ones = jnp.ones((SB_DH, SB_DH), BF16)
    for i in range(pp):
        z = _lane_group_sums(k_refs[i][0] * q, ones)
        lb, l1m = _log_sig_pair(z)
        lb_sc[...] = lb
        l1m_sc[...] = l1m

        def back(j, carry, v_ref=v_refs[i]):
            acc, run = carry
            s = PAGE - 1 - j
            acc = acc + jnp.exp(lb_sc[s] + run) * v_ref[0, s]
            return acc, run + l1m_sc[s]

        acc, run = lax.fori_loop(0, PAGE, back, (acc_sc[...], run_sc[...]), unroll=8)
        acc_sc[...] = acc
        run_sc[...] = run

    @pl.when(p == pl.num_programs(1) - 1)
    def _():
        o_ref[0] = acc_sc[...]


def sb_decode(sq, cache_k, cache_v, page_table, pp=4):
    b = sq.shape[0]
    n_pages = page_table.shape[1]
    steps = n_pages // pp
    hd = (SB_HEADS, SB_DH)

    def page_spec(i):
        return pl.BlockSpec((1, PAGE) + hd,
                            lambda bi, p, pt: (pt[bi * n_pages + (n_pages - 1 - (p * pp + i))], 0, 0, 0))

    grid_spec = pltpu.PrefetchScalarGridSpec(
        num_scalar_prefetch=1,
        grid=(b, steps),
        in_specs=[pl.BlockSpec((1,) + hd, lambda bi, p, pt: (bi, 0, 0))]
                 + [page_spec(i) for i in range(pp)] + [page_spec(i) for i in range(pp)],
        out_specs=pl.BlockSpec((1,) + hd, lambda bi, p, pt: (bi, 0, 0)),
        scratch_shapes=[pltpu.VMEM(hd, F32), pltpu.VMEM(hd, F32),
                        pltpu.VMEM((PAGE,) + hd, F32), pltpu.VMEM((PAGE,) + hd, F32)],
    )
    out = pl.pallas_call(
        functools.partial(_sb_decode_kernel, pp=pp),
        grid_spec=grid_spec,
        out_shape=jax.ShapeDtypeStruct((b,) + hd, F32),
        compiler_params=_params(("arbitrary", "arbitrary")),
        name="sb_decode",
    )(page_table.reshape(-1), sq.reshape((b,) + hd), *([cache_k] * pp), *([cache_v] * pp))
    return out.reshape(b, SB_HEADS * SB_DH)


def _lambda_value(lam_ref):
    lp = lam_ref[...]
    s1 = jnp.sum(lp[0:1, :] * lp[1:2, :], axis=1, keepdims=True)
    s2 = jnp.sum(lp[2:3, :] * lp[3:4, :], axis=1, keepdims=True)
    return jnp.exp(s1) - jnp.exp(s2) + DF_LAMBDA_INIT


def _diff_prompt_kernel(q_ref, k_ref, v_ref, lam_ref, nw_ref, o_ref, *, bq):
    qi = pl.program_id(2)
    lane = lax.broadcasted_iota(I32, (bq, LANES), 1)
    q = q_ref[...] * (DF_D ** -0.5)
    qc = [jnp.where((lane // DF_D) == c, q, 0.0).astype(BF16) for c in range(2)]
    row = lax.broadcasted_iota(I32, (bq, bq), 0)
    colm = lax.broadcasted_iota(I32, (bq, bq), 1)
    causal = colm <= row

    def tile(kb, carry, masked):
        start = pl.multiple_of(kb * bq, bq)
        kblk = k_ref[pl.ds(start, bq), :].astype(BF16)
        vblk = v_ref[pl.ds(start, bq), :].astype(BF16)
        out = []
        for c in range(2):
            m_prev, l_prev, acc = carry[c]
            z = lax.dot_general(qc[c], kblk, (((1,), (1,)), ((), ())), preferred_element_type=F32)
            if masked:
                z = jnp.where(causal, z, NEG_INF)
            m_new = jnp.maximum(m_prev, jnp.max(z, axis=1, keepdims=True))
            p = jnp.exp(z - m_new)
            alpha = jnp.exp(m_prev - m_new)
            l_new = alpha * l_prev + jnp.sum(p, axis=1, keepdims=True)
            acc = alpha * acc + jnp.dot(p.astype(BF16), vblk, preferred_element_type=F32)
            out.append((m_new, l_new, acc))
        return tuple(out)

    init = (jnp.full((bq, 1), NEG_INF, F32), jnp.zeros((bq, 1), F32), jnp.zeros((bq, LANES), F32))
    carry = tile(qi, (init, init), True)
    carry = lax.fori_loop(0, qi, lambda i, c: tile(qi - 1 - i, c, False), carry)
    lam = _lambda_value(lam_ref)
    o = carry[0][2] / carry[0][1] - lam * (carry[1][2] / carry[1][1])
    o_ref[...] = _rms(o, nw_ref[...]) * (1.0 - DF_LAMBDA_INIT)


def diff_prompt(q, k, v, lam_pack, norm_w, batch, seq, bq=256):
    nq = seq // bq
    return pl.pallas_call(
        functools.partial(_diff_prompt_kernel, bq=bq),
        grid=(batch, DF_HEADS, nq),
        in_specs=[pl.BlockSpec((bq, LANES), lambda b, h, i: (b * nq + i, h)),
                  pl.BlockSpec((seq, LANES), lambda b, h, i: (b, h)),
                  pl.BlockSpec((seq, LANES), lambda b, h, i: (b, h)),
                  pl.BlockSpec((8, DF_D), lambda b, h, i: (0, 0)),
                  pl.BlockSpec((1, DF_V), lambda b, h, i: (0, 0))],
        out_specs=pl.BlockSpec((bq, LANES), lambda b, h, i: (b * nq + i, h)),
        out_shape=jax.ShapeDtypeStruct((batch * seq, DF_HEADS * DF_V), F32),
        compiler_params=_params(("arbitrary", "arbitrary", "arbitrary")),
        name="diff_prompt",
    )(q, k, v, lam_pack, norm_w.reshape(1, DF_V))


def _diff_decode_kernel(pt_ref, q_ref, kn_ref, vn_ref, lam_ref, nw_ref, *refs, pp):
    k_refs = refs[:pp]
    v_refs = refs[pp:2 * pp]
    o_ref = refs[2 * pp]
    m_sc, l_sc, acc_sc = refs[2 * pp + 1:]
    p = pl.program_id(1)
    li = lax.broadcasted_iota(I32, (DF_V, 2 * DF_V), 0)
    ni = lax.broadcasted_iota(I32, (DF_V, 2 * DF_V), 1)
    ind = ((li // DF_D) == (ni // DF_V)).astype(BF16)
    q = q_ref[0] * (DF_D ** -0.5)

    @pl.when(p == 0)
    def _():
        z_own = _lane_group_sums((kn_ref[0] * q)[None], ind)[0]
        for c in range(2):
            m_sc[c] = z_own[:, c * DF_V:(c + 1) * DF_V]
            l_sc[c] = jnp.ones((DF_HEADS, DF_V), F32)
            acc_sc[c] = vn_ref[0]

    for i in range(pp):
        z = _lane_group_sums(k_refs[i][0] * q, ind)
        v = v_refs[i][0]
        for c in range(2):
            zc = z[:, :, c * DF_V:(c + 1) * DF_V]
            m_prev = m_sc[c]
            m_new = jnp.maximum(m_prev, jnp.max(zc, axis=0))
            pr = jnp.exp(zc - m_new)
            alpha = jnp.exp(m_prev - m_new)
            l_sc[c] = alpha * l_sc[c] + jnp.sum(pr, axis=0)
            acc_sc[c] = alpha * acc_sc[c] + jnp.sum(pr * v, axis=0)
            m_sc[c] = m_new

    @pl.when(p == pl.num_programs(1) - 1)
    def _():
        lam = _lambda_value(lam_ref)
        o = acc_sc[0] / l_sc[0] - lam * (acc_sc[1] / l_sc[1])
        o_ref[0] = _rms(o, nw_ref[...]) * (1.0 - DF_LAMBDA_INIT)


def diff_decode(q, k_new, v_new, cache_k, cache_v, page_table, lam_pack, norm_w, pp=4):
    b = q.shape[0]
    n_pages = page_table.shape[1]
    steps = n_pages // pp
    hd = (DF_HEADS, DF_V)

    def page_spec(i):
        return pl.BlockSpec((1, PAGE) + hd, lambda bi, p, pt: (pt[bi * n_pages + p * pp + i], 0, 0, 0))

    row = pl.BlockSpec((1,) + hd, lambda bi, p, pt: (bi, 0, 0))
    grid_spec = pltpu.PrefetchScalarGridSpec(
        num_scalar_prefetch=1,
        grid=(b, steps),
        in_specs=[row, row, row,
                  pl.BlockSpec((8, DF_D), lambda bi, p, pt: (0, 0)),
                  pl.BlockSpec((1, DF_V), lambda bi, p, pt: (0, 0))]
                 + [page_spec(i) for i in range(pp)] + [page_spec(i) for i in range(pp)],
        out_specs=row,
        scratch_shapes=[pltpu.VMEM((2,) + hd, F32), pltpu.VMEM((2,) + hd, F32), pltpu.VMEM((2,) + hd, F32)],
    )
    out = pl.pallas_call(
        functools.partial(_diff_decode_kernel, pp=pp),
        grid_spec=grid_spec,
        out_shape=jax.ShapeDtypeStruct((b,) + hd, F32),
        compiler_params=_params(("arbitrary", "arbitrary")),
        name="diff_decode",
    )(page_table.reshape(-1), q.reshape((b,) + hd), k_new.reshape((b,) + hd), v_new.reshape((b,) + hd),
      lam_pack, norm_w.reshape(1, DF_V), *([cache_k] * pp), *([cache_v] * pp))
    return out.reshape(b, DF_HEADS * DF_V)


IDX_NONE = 1e9


def _top_rows(tiles, ridx, k):
    n = tiles[0].shape[1]
    kio = lax.broadcasted_iota(I32, (k, n), 0)

    def body(i, carry):
        cur, vals, idxs = carry
        m = cur[0]
        for t in cur[1:]:
            m = jnp.maximum(m, t)
        m = jnp.max(m, axis=0, keepdims=True)
        f = None
        for t, ix in zip(cur, ridx):
            c = jnp.where(t == m, ix, IDX_NONE)
            f = c if f is None else jnp.minimum(f, c)
        f = jnp.min(f, axis=0, keepdims=True)
        cur = tuple(jnp.where(ix == f, NEG_INF, t) for t, ix in zip(cur, ridx))
        vals = jnp.where(kio == i, m, vals)
        idxs = jnp.where(kio == i, f, idxs)
        return cur, vals, idxs

    init = (tuple(tiles), jnp.zeros((k, n), F32), jnp.zeros((k, n), F32))
    _, vals, idxs = lax.fori_loop(0, k, body, init)
    return vals, idxs


def _pair_candidates(s1, s2):
    k, n = s1.shape
    sub = 8
    r8 = lax.broadcasted_iota(I32, (sub, n), 0)
    r8f = r8.astype(F32)
    tiles, ridx = [], []
    for a in range(k):
        nb = k // (a + 1)
        if nb == 1:
            break
        for b0 in range(0, nb, sub):
            t = s1[a:a + 1, :] + s2[b0:b0 + sub, :]
            ix = r8f + float(a * k + b0)
            if b0 + sub > nb:
                keep = r8 < (nb - b0)
                t = jnp.where(keep, t, NEG_INF)
                ix = jnp.where(keep, ix, IDX_NONE)
            tiles.append(t)
            ridx.append(ix)
    a0 = a
    assert (k - a0) % sub == 0
    for a1 in range(a0, k, sub):
        tiles.append(s1[a1:a1 + sub, :] + s2[0:1, :])
        ridx.append((r8f + float(a1)) * float(k))
    return tiles, ridx


def _pick_rows(table, sel):
    kk = table.shape[0]
    out = jnp.zeros(sel.shape, F32)
    for a in range(kk):
        out = jnp.where(sel == float(a), table[a:a + 1, :], out)
    return out


def _peer_select_kernel(h_ref, nw_ref, wq_ref, keys_ref, xn_ref, e_ref, g_ref, e_sc):
    kk = PEER_TOPK
    xn = _rms(h_ref[...], nw_ref[...])
    xn_ref[...] = xn
    q = jnp.dot(xn.astype(BF16), wq_ref[...], preferred_element_type=F32)
    tb = q.shape[0]
    key_idx = [lax.broadcasted_iota(I32, (PEER_NKEYS, tb), 0).astype(F32)]
    for h in range(PEER_HEADS):
        tops = []
        for c in range(2):
            j = h * 2 + c
            s_t = _bdot_nt(keys_ref[c], q[:, j * LANES:(j + 1) * LANES])
            tops.append(_top_rows([s_t], key_idx, kk))
        (s1, i1), (s2, i2) = tops
        c_top, c_idx = _top_rows(*_pair_candidates(s1, s2), kk)
        ia = jnp.floor(c_idx * (1.0 / kk))
        ib = c_idx - ia * kk
        e = _pick_rows(i1, ia) * float(PEER_NKEYS) + _pick_rows(i2, ib)
        ex = jnp.exp(c_top - jnp.max(c_top, axis=0, keepdims=True))
        g = ex / jnp.sum(ex, axis=0, keepdims=True)
        e_sc[h * kk:(h + 1) * kk, :] = e
        for blk in range(tb // LANES):
            g_ref[blk, h * kk:(h + 1) * kk, :] = g[:, blk * LANES:(blk + 1) * LANES]
    e_ref[...] = e_sc[...].T.astype(I32)


def peer_select(h, nw, wq_bf16, keys):
    t, d = h.shape
    tb = 2 * LANES if t % (2 * LANES) == 0 else LANES
    picks = PEER_HEADS * PEER_TOPK
    gb = tb // LANES
    return pl.pallas_call(
        _peer_select_kernel,
        grid=(t // tb,),
        in_specs=[pl.BlockSpec((tb, d), lambda i: (i, 0)),
                  pl.BlockSpec((1, d), lambda i: (0, 0)),
                  pl.BlockSpec(wq_bf16.shape, lambda i: (0, 0)),
                  pl.BlockSpec(keys.shape, lambda i: (0, 0, 0))],
        out_specs=[pl.BlockSpec((tb, d), lambda i: (i, 0)),
                   pl.BlockSpec((tb, picks), lambda i: (i, 0)),
                   pl.BlockSpec((gb, picks, LANES), lambda i: (i, 0, 0))],
        out_shape=[jax.ShapeDtypeStruct((t, d), F32),
                   jax.ShapeDtypeStruct((t, picks), I32),
                   jax.ShapeDtypeStruct((t // LANES, picks, LANES), F32)],
        scratch_shapes=[pltpu.VMEM((picks, tb), F32)],
        compiler_params=_params(("arbitrary",)),
        name="peer_select",
    )(h, nw.reshape(1, d), wq_bf16, keys)


def _peer_gather_kernel(e_hbm, uv_hbm, xn_ref, g_ref, h_ref, o_ref, ids_smem, rows, ids_sem, row_sem,
                        *, n_tok, n_slots):
    i = pl.program_id(0)
    nblk = pl.num_programs(0)
    picks = PEER_HEADS * PEER_TOPK
    d = xn_ref.shape[1]
    tb = xn_ref.shape[0]
    ahead = n_slots - 1
    assert n_tok % n_slots == 0 and n_tok > ahead
    cur = i % 2
    has_next = i + 1 < nblk

    def ids_copy(blk, buf):
        return pltpu.make_async_copy(e_hbm.at[pl.ds(blk * tb, tb)], ids_smem.at[buf], ids_sem.at[buf])

    def issue(buf, t, slot, j0, j1):
        for j in range(j0, j1):
            e = ids_smem[buf, t, j]
            pltpu.make_async_copy(uv_hbm.at[e], rows.at[slot, pl.ds(j, 1)],
                                  row_sem.at[slot]).start(priority=j % 2)

    def wait(slot):
        pltpu.make_async_copy(rows.at[slot], rows.at[slot], row_sem.at[slot]).wait()

    lane = lax.broadcasted_iota(I32, (picks, LANES), 1)
    half = picks // 2

    def token(t, slot, issue_half):
        wait(slot)
        x = xn_ref[pl.ds(t, 1), :]
        prod = rows[slot, :, 0:d] * x
        part = prod[:, 0:LANES]
        for c in range(1, d // LANES):
            part = part + prod[:, c * LANES:(c + 1) * LANES]
        issue_half(0, half)
        act = jnp.sum(part, axis=1, keepdims=True)
        g = jnp.sum(jnp.where(lane == t, g_ref[0], 0.0), axis=1, keepdims=True)
        wgt = g * jax.nn.gelu(act)
        mix = jnp.sum(wgt * rows[slot, :, d:2 * d], axis=0, keepdims=True)
        issue_half(half, picks)
        o_ref[pl.ds(t, 1), :] = h_ref[pl.ds(t, 1), :] + mix

    @pl.when(i == 0)
    def _():
        first = ids_copy(0, 0)
        first.start()
        first.wait()
        for t in range(ahead):
            issue(0, t, t, 0, picks)

    @pl.when(has_next)
    def _():
        ids_copy(i + 1, 1 - cur).start()

    if n_tok < tb:
        o_ref[...] = h_ref[...]

    n_groups = n_tok // n_slots

    def group(gi, c):
        for u in range(n_slots):
            t = gi * n_slots + u
            token(t, u, functools.partial(issue, cur, t + ahead, (u + ahead) % n_slots))
        return c

    lax.fori_loop(0, n_groups - 1, group, 0)

    @pl.when(has_next)
    def _():
        ids_copy(i + 1, 1 - cur).wait()

    for u in range(n_slots):
        t = (n_groups - 1) * n_slots + u
        if u == 0:
            token(t, u, functools.partial(issue, cur, t + ahead, (u + ahead) % n_slots))
        else:
            def issue_next(j0, j1, u=u):
                @pl.when(has_next)
                def _():
                    issue(1 - cur, u - 1, (u + ahead) % n_slots, j0, j1)

            token(t, u, issue_next)


def peer_gather(e_ids, g_blocks, uv, xn, h, n_tok):
    t, d = h.shape
    tb = LANES
    nb = t // tb
    assert nb == 1 or n_tok == tb
    picks = PEER_HEADS * PEER_TOPK
    n_slots = 8
    return pl.pallas_call(
        functools.partial(_peer_gather_kernel, n_tok=n_tok, n_slots=n_slots),
        grid=(nb,),
        in_specs=[pl.BlockSpec(memory_space=pl.ANY),
                  pl.BlockSpec(memory_space=pl.ANY),
                  pl.BlockSpec((tb, d), lambda i: (i, 0)),
                  pl.BlockSpec((1, picks, tb), lambda i: (i, 0, 0)),
                  pl.BlockSpec((tb, d), lambda i: (i, 0))],
        out_specs=pl.BlockSpec((tb, d), lambda i: (i, 0)),
        out_shape=jax.ShapeDtypeStruct((t, d), F32),
        scratch_shapes=[pltpu.SMEM((2, tb, picks), I32),
                        pltpu.VMEM((n_slots, picks, 2 * d), F32),
                        pltpu.SemaphoreType.DMA((2,)),
                        pltpu.SemaphoreType.DMA((n_slots,))],
        compiler_params=_params(("arbitrary",)),
        name="peer_gather",
    )(e_ids, uv, xn, g_blocks, h)


def peer_layer(h, nw, wq_bf16, keys, uv):
    t = h.shape[0]
    tp = -(-t // LANES) * LANES
    hp = h if tp == t else jnp.pad(h, ((0, tp - t), (0, 0)))
    xn, e_ids, g_blocks = peer_select(hp, nw, wq_bf16, keys)
    out = peer_gather(e_ids, g_blocks, uv, xn, hp, min(t, LANES))
    return out if tp == t else out[:t]


def kernel(x_prompt, x_sample, p_prompt, p_sample, state_mlstm_C, state_mlstm_n, state_mlstm_m, cache_sb_k, cache_sb_v, cache_diff_k, cache_diff_v, page_table, norm_mix0, w_in0, b_igate0, b_fgate0, mlstm_norm0, w_out0, norm_ffn0, peer_wq0, peer_keys0, peer_u0, peer_v0, ple_gate0, ple_proj0, norm_mix1, w_qkv1, lambda_q1, lambda_k1, lambda_q2, lambda_k2, diff_norm1, w_out1, norm_ffn1, peer_wq1, peer_keys1, peer_u1, peer_v1, ple_gate1, ple_proj1, final_norm):
    bp, sp, d = x_prompt.shape
    bs, ss, _ = x_sample.shape
    assert ss == 1
    ml_w = ML_HEADS * ML_DH
    sb_w = SB_HEADS * SB_DH
    df_w = DF_HEADS * DF_V

    g0 = 4 * ml_w
    g1 = g0 + 2 * ML_HEADS
    gates_w = jnp.pad(w_in0[:, g0:g1], ((0, 0), (0, LANES - 2 * ML_HEADS)))
    w_in = jnp.concatenate([w_in0[:, :g0], w_in0[:, g1:], gates_w], axis=1).astype(BF16)
    in_splits = (4 * ml_w, sb_w, sb_w, sb_w, LANES)
    gate_bias = jnp.pad(jnp.concatenate([b_igate0, b_fgate0]), (0, LANES - 2 * ML_HEADS)).reshape(1, LANES)
    w_out0_b = w_out0.astype(BF16)
    w_qkv = w_qkv1.astype(BF16)
    w_out1_b = w_out1.astype(BF16)
    lam_pack = jnp.pad(jnp.stack([lambda_q1, lambda_k1, lambda_q2, lambda_k2]), ((0, 4), (0, 0)))
    uv0 = jnp.concatenate([peer_u0, peer_v0], axis=1).reshape(-1, 1, 2 * d)
    uv1 = jnp.concatenate([peer_u1, peer_v1], axis=1).reshape(-1, 1, 2 * d)
    peer0 = (norm_ffn0, peer_wq0.astype(BF16), peer_keys0, uv0)
    peer1 = (norm_ffn1, peer_wq1.astype(BF16), peer_keys1, uv1)
    ple0 = (ple_gate0.astype(BF16), ple_proj0.astype(BF16))
    ple1 = (ple_gate1.astype(BF16), ple_proj1.astype(BF16))

    def layer0_tail(h, mix_acts, p0):
        h = proj_residual(h, mix_acts, [w_out0_b[:ml_w], w_out0_b[ml_w:]])
        h = peer_layer(h, *peer0)
        return ple(h, p0, *ple0, final_norm, False)

    def layer1_tail(h, o, p1):
        h = proj_residual(h, [o], [w_out1_b])
        h = peer_layer(h, *peer1)
        return ple(h, p1, *ple1, final_norm, True)

    tp = bp * sp
    h = x_prompt.reshape(tp, d)
    ml, sq, sk, sv, gates = norm_proj(h, norm_mix0, w_in, in_splits)
    h_ml, c_p, n_p, m_p = mlstm_prompt(ml, gates, gate_bias, mlstm_norm0, bp, sp)
    h_sb = sb_prompt(sq, sk, sv, bp, sp)
    h = layer0_tail(h, [h_ml, h_sb], p_prompt[0].reshape(tp, -1))
    q1, k1, v1 = norm_proj(h, norm_mix1, w_qkv, (df_w, df_w, df_w))
    o = diff_prompt(q1, k1, v1, lam_pack, diff_norm1, bp, sp)
    y_prompt = layer1_tail(h, o, p_prompt[1].reshape(tp, -1)).reshape(bp, sp, d)

    hs = x_sample.reshape(bs, d)
    ml_s, sq_s, sk_s, sv_s, gates_s = norm_proj(hs, norm_mix0, w_in, in_splits)
    h_ml_s, c_s, n_s, m_s = mlstm_step(ml_s, gates_s, gate_bias, mlstm_norm0,
                                       state_mlstm_C, state_mlstm_n, state_mlstm_m)
    h_sb_s = sb_decode(sq_s, cache_sb_k, cache_sb_v, page_table)
    hs = layer0_tail(hs, [h_ml_s, h_sb_s], p_sample[0].reshape(bs, -1))
    q1s, k1s, v1s = norm_proj(hs, norm_mix1, w_qkv, (df_w, df_w, df_w))
    o_s = diff_decode(q1s, k1s, v1s, cache_diff_k, cache_diff_v, page_table, lam_pack, diff_norm1)
    y_sample = layer1_tail(hs, o_s, p_sample[1].reshape(bs, -1)).reshape(bs, ss, d)

    return (y_prompt, y_sample,
            c_p, n_p, m_p.reshape(bp, ML_HEADS),
            sk.reshape(bp, sp, SB_HEADS, SB_DH), sv.reshape(bp, sp, SB_HEADS, SB_DH),
            k1.reshape(bp, sp, DF_HEADS, DF_V), v1.reshape(bp, sp, DF_HEADS, DF_V),
            c_s, n_s, m_s,
            sk_s.reshape(bs, ss, SB_HEADS, SB_DH), sv_s.reshape(bs, ss, SB_HEADS, SB_DH),
            k1s.reshape(bs, ss, DF_HEADS, DF_V), v1s.reshape(bs, ss, DF_HEADS, DF_V))
```

```python
import functools
import math

import jax
import jax.numpy as jnp
from jax import lax
from jax.experimental import pallas as pl
from jax.experimental.pallas import tpu as pltpu

F32 = jnp.float32
BF16 = jnp.bfloat16
I32 = jnp.int32

RMS_EPS = 1e-6
LANES = 128
ML_HEADS, ML_DH, ML_CHUNK = 4, 128, 64
SB_HEADS, SB_DH = 8, 64
DF_HEADS, DF_D, DF_V = 8, 64, 128
DF_LAMBDA_INIT = 0.8 - 0.6 * math.exp(-0.3 * 1)
PEER_HEADS, PEER_NKEYS, PEER_TOPK = 8, 128, 16
PAGE = 128
VMEM_LIMIT = 56 * 1024 * 1024
NEG_INF = float("-inf")


def _params(sem):
    return pltpu.CompilerParams(dimension_semantics=sem, vmem_limit_bytes=VMEM_LIMIT)


def _bdot(a, b):
    return jnp.dot(a.astype(BF16), b.astype(BF16), preferred_element_type=F32)


def _bdot_nt(a, b):
    return lax.dot_general(a.astype(BF16), b.astype(BF16), (((1,), (1,)), ((), ())),
                           preferred_element_type=F32)


def _rms(x, w):
    return x * lax.rsqrt(jnp.mean(x * x, axis=-1, keepdims=True) + RMS_EPS) * w


def _row_tile(t):
    for tm in (256, 128, 64, 32, 16, 8):
        if t % tm == 0:
            return tm
    raise ValueError(f"row count {t} must be a multiple of 8")


def _norm_proj_kernel(x_ref, nw_ref, w_ref, *o_refs, splits):
    xn = _rms(x_ref[...], nw_ref[...]).astype(BF16)
    off = 0
    for o_ref, n in zip(o_refs, splits):
        o_ref[...] = jnp.dot(xn, w_ref[:, off:off + n], preferred_element_type=F32)
        off += n


def norm_proj(x, nw, w_bf16, splits):
    t, d = x.shape
    n = w_bf16.shape[1]
    assert sum(splits) == n and all(s % LANES == 0 for s in splits)
    tm = _row_tile(t)
    return pl.pallas_call(
        functools.partial(_norm_proj_kernel, splits=tuple(splits)),
        grid=(t // tm,),
        in_specs=[pl.BlockSpec((tm, d), lambda i: (i, 0)),
                  pl.BlockSpec((1, d), lambda i: (0, 0)),
                  pl.BlockSpec((d, n), lambda i: (0, 0))],
        out_specs=[pl.BlockSpec((tm, s), lambda i: (i, 0)) for s in splits],
        out_shape=[jax.ShapeDtypeStruct((t, s), F32) for s in splits],
        compiler_params=_params(("arbitrary",)),
        name="norm_proj",
    )(x, nw.reshape(1, d), w_bf16)


def _proj_res_kernel(*refs, n_in):
    res_ref = refs[0]
    a_refs = refs[1:1 + n_in]
    w_refs = refs[1 + n_in:1 + 2 * n_in]
    o_ref = refs[1 + 2 * n_in]
    acc = res_ref[...]
    for a_ref, w_ref in zip(a_refs, w_refs):
        acc = acc + jnp.dot(a_ref[...].astype(BF16), w_ref[...], preferred_element_type=F32)
    o_ref[...] = acc


def proj_residual(res, acts, ws_bf16):
    t, d = res.shape
    tm = _row_tile(t)
    n_in = len(acts)
    in_specs = [pl.BlockSpec((tm, d), lambda i: (i, 0))]
    in_specs += [pl.BlockSpec((tm, a.shape[1]), lambda i: (i, 0)) for a in acts]
    in_specs += [pl.BlockSpec(w.shape, lambda i: (0, 0)) for w in ws_bf16]
    return pl.pallas_call(
        functools.partial(_proj_res_kernel, n_in=n_in),
        grid=(t // tm,),
        in_specs=in_specs,
        out_specs=pl.BlockSpec((tm, d), lambda i: (i, 0)),
        out_shape=jax.ShapeDtypeStruct((t, d), F32),
        compiler_params=_params(("arbitrary",)),
        name="proj_residual",
    )(res, *acts, *ws_bf16)


def _ple_kernel(h_ref, p_ref, wg_ref, wp_ref, fn_ref, o_ref, *, final_norm):
    h = h_ref[...]
    gate = jax.nn.sigmoid(jnp.dot(h.astype(BF16), wg_ref[...], preferred_element_type=F32))
    emb = jnp.dot(p_ref[...].astype(BF16), wp_ref[...], preferred_element_type=F32)
    y = h + gate * emb
    if final_norm:
        y = _rms(y, fn_ref[...])
    o_ref[...] = y


def ple(h, p, wg_bf16, wp_bf16, fnorm, final_norm):
    t, d = h.shape
    pd = p.shape[1]
    tm = _row_tile(t)
    return pl.pallas_call(
        functools.partial(_ple_kernel, final_norm=final_norm),
        grid=(t // tm,),
        in_specs=[pl.BlockSpec((tm, d), lambda i: (i, 0)),
                  pl.BlockSpec((tm, pd), lambda i: (i, 0)),
                  pl.BlockSpec((d, d), lambda i: (0, 0)),
                  pl.BlockSpec((pd, d), lambda i: (0, 0)),
                  pl.BlockSpec((1, d), lambda i: (0, 0))],
        out_specs=pl.BlockSpec((tm, d), lambda i: (i, 0)),
        out_shape=jax.ShapeDtypeStruct((t, d), F32),
        compiler_params=_params(("arbitrary",)),
        name="ple",
    )(h, p, wg_bf16, wp_bf16, fnorm.reshape(1, d))


def _mlstm_chunk_kernel(q_ref, k_ref, v_ref, o_ref, g_ref, gb_ref, nw_ref,
                        h_ref, c_out, n_out, m_out, c_sc, n_sc, m_sc):
    ci = pl.program_id(1)
    L = ML_CHUNK

    @pl.when(ci == 0)
    def _():
        c_sc[...] = jnp.zeros_like(c_sc)
        n_sc[...] = jnp.zeros_like(n_sc)
        m_sc[...] = jnp.zeros_like(m_sc)

    hp = lax.Precision.HIGHEST
    g = g_ref[...] + gb_ref[...]
    lf = jax.nn.log_sigmoid(g)
    ri = lax.broadcasted_iota(I32, (L, L), 0)
    cj = lax.broadcasted_iota(I32, (L, L), 1)
    tril = (cj <= ri)
    low = tril.astype(F32)
    upp = (ri <= cj).astype(F32)
    eye = (lax.broadcasted_iota(I32, (LANES, LANES), 0) ==
           lax.broadcasted_iota(I32, (LANES, LANES), 1)).astype(F32)
    nt = (((1,), (1,)), ((), ()))
    g_t = lax.dot_general(eye, g, nt, precision=hp, preferred_element_type=F32)
    lf_t = lax.dot_general(eye, lf, nt, precision=hp, preferred_element_type=F32)
    b_c = jnp.dot(low, lf, precision=hp, preferred_element_type=F32)
    b_r = jnp.dot(lf_t, upp, precision=hp, preferred_element_type=F32)

    for h in range(ML_HEADS):
        sl = slice(h * ML_DH, (h + 1) * ML_DH)
        q = q_ref[:, sl] * (ML_DH ** -0.5)
        k = k_ref[:, sl]
        v = v_ref[:, sl]
        li_c = g[:, h:h + 1]
        li_r = g_t[h:h + 1, :]
        bc = b_c[:, ML_HEADS + h:ML_HEADS + h + 1]
        br = b_r[ML_HEADS + h:ML_HEADS + h + 1, :]
        m_prev = m_sc[:, h:h + 1]
        c_prev = c_sc[h]
        n_prev = n_sc[h:h + 1, :]
        dmat = jnp.where(tril, bc - br + li_r, NEG_INF)
        inter = bc + m_prev
        m_t = jnp.maximum(inter, jnp.max(dmat, axis=1, keepdims=True))
        w_inter = jnp.exp(inter - m_t)
        wqk = jnp.exp(dmat - m_t) * _bdot_nt(q, k)
        num = w_inter * _bdot_nt(q, c_prev) + _bdot(wqk, v)
        den = w_inter * jnp.sum(q * n_prev, axis=1, keepdims=True) + jnp.sum(wqk, axis=1, keepdims=True)
        hh = num / jnp.maximum(jnp.abs(den), jnp.exp(-m_t))
        b_last = br[:, L - 1:L]
        g_r = b_last - br + li_r
        g_c = b_last - bc + li_c
        m_new = jnp.maximum(b_last + m_prev, jnp.max(g_r, axis=1, keepdims=True))
        a = jnp.exp(b_last + m_prev - m_new)
        wg_c = jnp.exp(g_c - m_new)
        wv = (wg_c * v).astype(BF16)
        c_new = a * c_prev + lax.dot_general(wv, k.astype(BF16), (((0,), (0,)), ((), ())),
                                             preferred_element_type=F32)
        n_new = a * n_prev + jnp.sum(wg_c * k, axis=0, keepdims=True)
        c_sc[h] = c_new
        n_sc[h:h + 1, :] = n_new
        m_sc[:, h:h + 1] = m_new
        hn = _rms(hh, nw_ref[:, sl])
        h_ref[:, sl] = jax.nn.sigmoid(o_ref[:, sl]) * hn

    @pl.when(ci == pl.num_programs(1) - 1)
    def _():
        c_out[0] = c_sc[...]
        n_out[0] = n_sc[...]
        m_out[0] = m_sc[:, 0:ML_HEADS]


def mlstm_prompt(ml, gates, gate_bias, norm_w, batch, seq):
    L = ML_CHUNK
    nc = seq // L
    w = ML_HEADS * ML_DH
    col = lambda j: pl.BlockSpec((L, w), lambda b, c: (b * nc + c, j))
    return pl.pallas_call(
        _mlstm_chunk_kernel,
        grid=(batch, nc),
        in_specs=[col(0), col(1), col(2), col(3),
                  pl.BlockSpec((L, LANES), lambda b, c: (b * nc + c, 0)),
                  pl.BlockSpec((1, LANES), lambda b, c: (0, 0)),
                  pl.BlockSpec((1, w), lambda b, c: (0, 0))],
        out_specs=[pl.BlockSpec((L, w), lambda b, c: (b * nc + c, 0)),
                   pl.BlockSpec((1, ML_HEADS, ML_DH, ML_DH), lambda b, c: (b, 0, 0, 0)),
                   pl.BlockSpec((1, ML_HEADS, ML_DH), lambda b, c: (b, 0, 0)),
                   pl.BlockSpec((1, 1, ML_HEADS), lambda b, c: (b, 0, 0))],
        out_shape=[jax.ShapeDtypeStruct((batch * seq, w), F32),
                   jax.ShapeDtypeStruct((batch, ML_HEADS, ML_DH, ML_DH), F32),
                   jax.ShapeDtypeStruct((batch, ML_HEADS, ML_DH), F32),
                   jax.ShapeDtypeStruct((batch, 1, ML_HEADS), F32)],
        scratch_shapes=[pltpu.VMEM((ML_HEADS, ML_DH, ML_DH), F32),
                        pltpu.VMEM((ML_HEADS, ML_DH), F32),
                        pltpu.VMEM((1, LANES), F32)],
        compiler_params=_params(("arbitrary", "arbitrary")),
        name="mlstm_chunk",
    )(ml, ml, ml, ml, gates, gate_bias, norm_w.reshape(1, w))


def _mlstm_step_kernel(x_ref, g_ref, gb_ref, nw_ref, c_ref, n_ref, m_ref,
                       h_ref, c_out, n_out, m_out):
    w = ML_HEADS * ML_DH
    g = g_ref[0] + gb_ref[...]
    lf = jax.nn.log_sigmoid(g)
    eye = (lax.broadcasted_iota(I32, (ML_DH, ML_DH), 0) ==
           lax.broadcasted_iota(I32, (ML_DH, ML_DH), 1)).astype(F32)

    def to_col(row):
        return jnp.sum(eye * row, axis=1, keepdims=True)

    def to_row(colv):
        return jnp.sum(eye * colv, axis=0, keepdims=True)

    x = x_ref[0]
    for h in range(ML_HEADS):
        q = x[:, h * ML_DH:(h + 1) * ML_DH] * (ML_DH ** -0.5)
        k = x[:, w + h * ML_DH:w + (h + 1) * ML_DH]
        v = x[:, 2 * w + h * ML_DH:2 * w + (h + 1) * ML_DH]
        o = x[:, 3 * w + h * ML_DH:3 * w + (h + 1) * ML_DH]
        li = g[:, h:h + 1]
        lfh = lf[:, ML_HEADS + h:ML_HEADS + h + 1]
        m_prev = m_ref[0][:, h:h + 1]
        c_prev = c_ref[0, h]
        n_prev = n_ref[0][h:h + 1, :]
        inter = lfh + m_prev
        m_t = jnp.maximum(inter, li)
        w_inter = jnp.exp(inter - m_t)
        wqk = jnp.exp(li - m_t) * jnp.sum(q * k, axis=1, keepdims=True)
        v_col = to_col(v)
        num = w_inter * jnp.sum(c_prev * q, axis=1, keepdims=True) + wqk * v_col
        den = w_inter * jnp.sum(n_prev * q, axis=1, keepdims=True) + wqk
        hh = num / jnp.maximum(jnp.abs(den), jnp.exp(-m_t))
        m_new = m_t
        a = jnp.exp(inter - m_new)
        wg = jnp.exp(li - m_new)
        c_out[0, h] = a * c_prev + (wg * v_col) * k
        n_out[0, h:h + 1, :] = a * n_prev + wg * k
        m_out[0, :, h:h + 1] = m_new
        hr = to_row(hh)
        hn = _rms(hr, nw_ref[:, h * ML_DH:(h + 1) * ML_DH])
        h_ref[0, :, h * ML_DH:(h + 1) * ML_DH] = jax.nn.sigmoid(o) * hn


def mlstm_step(ml, gates, gate_bias, norm_w, c0, n0, m0):
    b = ml.shape[0]
    w = ML_HEADS * ML_DH
    row = lambda n: pl.BlockSpec((1, 1, n), lambda i: (i, 0, 0))
    h, c, n, m = pl.pallas_call(
        _mlstm_step_kernel,
        grid=(b,),
        in_specs=[row(4 * w), row(LANES),
                  pl.BlockSpec((1, LANES), lambda i: (0, 0)),
                  pl.BlockSpec((1, w), lambda i: (0, 0)),
                  pl.BlockSpec((1, ML_HEADS, ML_DH, ML_DH), lambda i: (i, 0, 0, 0)),
                  pl.BlockSpec((1, ML_HEADS, ML_DH), lambda i: (i, 0, 0)),
                  row(ML_HEADS)],
        out_specs=[row(w),
                   pl.BlockSpec((1, ML_HEADS, ML_DH, ML_DH), lambda i: (i, 0, 0, 0)),
                   pl.BlockSpec((1, ML_HEADS, ML_DH), lambda i: (i, 0, 0)),
                   row(ML_HEADS)],
        out_shape=[jax.ShapeDtypeStruct((b, 1, w), F32),
                   jax.ShapeDtypeStruct((b, ML_HEADS, ML_DH, ML_DH), F32),
                   jax.ShapeDtypeStruct((b, ML_HEADS, ML_DH), F32),
                   jax.ShapeDtypeStruct((b, 1, ML_HEADS), F32)],
        compiler_params=_params(("arbitrary",)),
        name="mlstm_step",
    )(ml.reshape(b, 1, 4 * w), gates.reshape(b, 1, LANES), gate_bias, norm_w.reshape(1, w),
      c0, n0, m0.reshape(b, 1, ML_HEADS))
    return h.reshape(b, w), c, n, m.reshape(b, ML_HEADS)


def _log_sig_pair(z):
    lg = jnp.log(1.0 + jnp.exp(-jnp.abs(z)))
    lb = jnp.minimum(z, 0.0) - lg
    return lb, lb - z


def _suffix_sum(l1m, tri_bf16):
    hi = l1m.astype(BF16)
    lo = (l1m - hi.astype(F32)).astype(BF16)
    return (jnp.dot(hi, tri_bf16, preferred_element_type=F32) +
            jnp.dot(lo, tri_bf16, preferred_element_type=F32))


def _wavefront(n_chains, stages):
    vals = [None] * n_chains
    for w in range(len(stages) + n_chains - 1):
        for c in range(n_chains):
            s = w - c
            if 0 <= s < len(stages):
                vals[c] = stages[s](c, vals[c])
    return vals


def _causal_schedule(nsub):
    return [(j, [(s, s == j) for s in range(j, nsub)]) for j in range(nsub - 1, -1, -1)]


def _sb_prompt_kernel(q_ref, k_ref, v_ref, tri_ref, o_ref, *, bk, nsub):
    base = pl.program_id(2) * nsub
    lane = lax.broadcasted_iota(I32, (bk, LANES), 1)
    tri = tri_ref[...]
    row = lax.broadcasted_iota(I32, (bk, bk), 0)
    colm = lax.broadcasted_iota(I32, (bk, bk), 1)
    causal = colm < row
    qops = []
    for s in range(nsub):
        q = q_ref[s * bk:(s + 1) * bk, :] * (SB_DH ** -0.5)
        qops.append([jnp.where((lane // SB_DH) == h, q, 0.0).astype(BF16) for h in range(2)])

    def run(kb, state, active):
        start = pl.multiple_of(kb * bk, bk)
        kblk = k_ref[pl.ds(start, bk), :].astype(BF16)
        vblk = v_ref[pl.ds(start, bk), :].astype(BF16)
        chains = [(s, h, m) for (s, m) in active for h in range(2)]

        def scores(c, _):
            s, h, _m = chains[c]
            return lax.dot_general(qops[s][h], kblk, (((1,), (1,)), ((), ())), preferred_element_type=F32)

        def logs(c, z):
            lb, l1m = _log_sig_pair(z)
            if chains[c][2]:
                l1m = jnp.where(causal, l1m, 0.0)
            hi = l1m.astype(BF16)
            return lb, l1m, hi, (l1m - hi.astype(F32)).astype(BF16)

        def suffix(c, v):
            lb, l1m, hi, lo = v
            rev = jnp.dot(hi, tri, preferred_element_type=F32) + jnp.dot(lo, tri, preferred_element_type=F32)
            return lb, l1m, rev

        def weights(c, v):
            s, h, m = chains[c]
            lb, l1m, rev = v
            tot = state[s * 2 + h][1]
            a = jnp.exp(lb + rev + tot)
            if m:
                a = jnp.where(causal, a, 0.0)
            return a.astype(BF16), tot + rev[:, 0:1] + l1m[:, 0:1]

        def values(c, v):
            s, h, _m = chains[c]
            a, tot = v
            return state[s * 2 + h][0] + jnp.dot(a, vblk, preferred_element_type=F32), tot

        outs = _wavefront(len(chains), [scores, logs, suffix, weights, values])
        new_state = list(state)
        for c, (s, h, _m) in enumerate(chains):
            new_state[s * 2 + h] = outs[c]
        return tuple(new_state)

    state = tuple((jnp.zeros((bk, LANES), F32), jnp.zeros((bk, 1), F32)) for _ in range(2 * nsub))
    for j, active in _causal_schedule(nsub):
        state = run(base + j, state, active)
    everyone = [(s, False) for s in range(nsub)]
    state = lax.fori_loop(0, base, lambda i, st: run(base - 1 - i, st, everyone), state)
    for s in range(nsub):
        o_ref[s * bk:(s + 1) * bk, :] = jnp.where(lane < SB_DH, state[2 * s][0], state[2 * s + 1][0])


def sb_prompt(sq, sk, sv, batch, seq, bk=256, nsub=2):
    bq = bk * nsub
    nq = seq // bq
    pairs = SB_HEADS // 2
    ri = lax.broadcasted_iota(I32, (bk, bk), 0)
    ci = lax.broadcasted_iota(I32, (bk, bk), 1)
    tri = (ri > ci).astype(BF16)
    return pl.pallas_call(
        functools.partial(_sb_prompt_kernel, bk=bk, nsub=nsub),
        grid=(batch, pairs, nq),
        in_specs=[pl.BlockSpec((bq, LANES), lambda b, p, i: (b * nq + i, p)),
                  pl.BlockSpec((seq, LANES), lambda b, p, i: (b, p)),
                  pl.BlockSpec((seq, LANES), lambda b, p, i: (b, p)),
                  pl.BlockSpec((bk, bk), lambda b, p, i: (0, 0))],
        out_specs=pl.BlockSpec((bq, LANES), lambda b, p, i: (b * nq + i, p)),
        out_shape=jax.ShapeDtypeStruct((batch * seq, SB_HEADS * SB_DH), F32),
        compiler_params=_params(("arbitrary", "arbitrary", "arbitrary")),
        name="sb_prompt",
    )(sq, sk, sv, tri)


def _lane_group_sums(x, ind_bf16):
    n, s, l = x.shape
    x2 = x.reshape(n * s, l)
    hi = x2.astype(BF16)
    lo = (x2 - hi.astype(F32)).astype(BF16)
    y = (jnp.dot(hi, ind_bf16, preferred_element_type=F32) +
         jnp.dot(lo, ind_bf16, preferred_element_type=F32))
    return y.reshape(n, s, ind_bf16.shape[1])


def _sb_decode_kernel(pt_ref, q_ref, *refs, pp):
    k_refs = refs[:pp]
    v_refs = refs[pp:2 * pp]
    tri_ref = refs[2 * pp]
    o_ref = refs[2 * pp + 1]
    acc_sc, tot_sc = refs[2 * pp + 2:]
    p = pl.program_id(1)

    @pl.when(p == 0)
    def _():
        acc_sc[...] = jnp.zeros_like(acc_sc)
        tot_sc[...] = jnp.zeros_like(tot_sc)

    tri = tri_ref[...]
    qcol = [q_ref[0, h] * (SB_DH ** -0.5) for h in range(SB_HEADS)]
    tot = tot_sc[...]
    for i in range(pp):
        z = jnp.concatenate([jnp.sum(k_refs[i][0, h] * qcol[h], axis=0, keepdims=True)
                             for h in range(SB_HEADS)], axis=0)
        lb, l1m = _log_sig_pair(z)
        rev = _suffix_sum(l1m, tri)
        a = jnp.exp(lb + rev + tot)
        tot = tot + rev[:, 0:1] + l1m[:, 0:1]
        for h in range(SB_HEADS):
            acc_sc[h] += v_refs[i][0, h] * a[h:h + 1, :]
    tot_sc[...] = tot

    @pl.when(p == pl.num_programs(1) - 1)
    def _():
        for h in range(SB_HEADS):
            o_ref[0, h] = jnp.sum(acc_sc[h], axis=1, keepdims=True)


def sb_decode(sq, cache_k, cache_v, page_table, pp=4):
    b = sq.shape[0]
    n_pages = page_table.shape[1]
    steps = n_pages // pp
    ck = jnp.transpose(cache_k, (0, 2, 3, 1))
    cv = jnp.transpose(cache_v, (0, 2, 3, 1))
    ri = lax.broadcasted_iota(I32, (PAGE, PAGE), 0)
    ci = lax.broadcasted_iota(I32, (PAGE, PAGE), 1)
    tri = (ri > ci).astype(BF16)

    def page_spec(i):
        return pl.BlockSpec((1, SB_HEADS, SB_DH, PAGE),
                            lambda bi, p, pt: (pt[bi * n_pages + (n_pages - 1 - (p * pp + i))], 0, 0, 0))

    col = pl.BlockSpec((1, SB_HEADS, SB_DH, 1), lambda bi, p, pt: (bi, 0, 0, 0))
    grid_spec = pltpu.PrefetchScalarGridSpec(
        num_scalar_prefetch=1,
        grid=(b, steps),
        in_specs=[col] + [page_spec(i) for i in range(pp)] + [page_spec(i) for i in range(pp)]
                 + [pl.BlockSpec((PAGE, PAGE), lambda bi, p, pt: (0, 0))],
        out_specs=col,
        scratch_shapes=[pltpu.VMEM((SB_HEADS, SB_DH, PAGE), F32), pltpu.VMEM((SB_HEADS, 1), F32)],
    )
    out = pl.pallas_call(
        functools.partial(_sb_decode_kernel, pp=pp),
        grid_spec=grid_spec,
        out_shape=jax.ShapeDtypeStruct((b, SB_HEADS, SB_DH, 1), F32),
        compiler_params=_params(("arbitrary", "arbitrary")),
        name="sb_decode",
    )(page_table.reshape(-1), sq.reshape(b, SB_HEADS, SB_DH, 1), *([ck] * pp), *([cv] * pp), tri)
    return out.reshape(b, SB_HEADS * SB_DH)


def _lambda_value(lam_ref):
    lp = lam_ref[...]
    s1 = jnp.sum(lp[0:1, :] * lp[1:2, :], axis=1, keepdims=True)
    s2 = jnp.sum(lp[2:3, :] * lp[3:4, :], axis=1, keepdims=True)
    return jnp.exp(s1) - jnp.exp(s2) + DF_LAMBDA_INIT


def _diff_prompt_kernel(q_ref, k_ref, v_ref, lam_ref, nw_ref, o_ref, *, bk, nsub):
    base = pl.program_id(2) * nsub
    lane = lax.broadcasted_iota(I32, (bk, LANES), 1)
    row = lax.broadcasted_iota(I32, (bk, bk), 0)
    colm = lax.broadcasted_iota(I32, (bk, bk), 1)
    causal = colm <= row
    qops = []
    for s in range(nsub):
        q = q_ref[s * bk:(s + 1) * bk, :] * (DF_D ** -0.5)
        qops.append([jnp.where((lane // DF_D) == c, q, 0.0).astype(BF16) for c in range(2)])

    def run(kb, state, active):
        start = pl.multiple_of(kb * bk, bk)
        kblk = k_ref[pl.ds(start, bk), :].astype(BF16)
        vblk = v_ref[pl.ds(start, bk), :].astype(BF16)
        chains = [(s, c, m) for (s, m) in active for c in range(2)]

        def scores(i, _):
            s, c, _m = chains[i]
            return lax.dot_general(qops[s][c], kblk, (((1,), (1,)), ((), ())), preferred_element_type=F32)

        def probs(i, z):
            s, c, m = chains[i]
            m_prev, l_prev, _ = state[s * 2 + c]
            if m:
                z = jnp.where(causal, z, NEG_INF)
            m_new = jnp.maximum(m_prev, jnp.max(z, axis=1, keepdims=True))
            p = jnp.exp(z - m_new)
            alpha = jnp.exp(m_prev - m_new)
            return m_new, alpha * l_prev + jnp.sum(p, axis=1, keepdims=True), alpha, p.astype(BF16)

        def values(i, v):
            m_new, l_new, alpha, p = v
            return m_new, l_new, alpha, jnp.dot(p, vblk, preferred_element_type=F32)

        def merge(i, v):
            s, c, _m = chains[i]
            m_new, l_new, alpha, pv = v
            return m_new, l_new, alpha * state[s * 2 + c][2] + pv

        outs = _wavefront(len(chains), [scores, probs, values, merge])
        new_state = list(state)
        for i, (s, c, _m) in enumerate(chains):
            new_state[s * 2 + c] = outs[i]
        return tuple(new_state)

    init = (jnp.full((bk, 1), NEG_INF, F32), jnp.zeros((bk, 1), F32), jnp.zeros((bk, LANES), F32))
    state = (init,) * (2 * nsub)
    for j, active in _causal_schedule(nsub):
        state = run(base + j, state, active)
    everyone = [(s, False) for s in range(nsub)]
    state = lax.fori_loop(0, base, lambda i, st: run(base - 1 - i, st, everyone), state)
    lam = _lambda_value(lam_ref)
    for s in range(nsub):
        (_, l0, a0), (_, l1, a1) = state[2 * s], state[2 * s + 1]
        o = a0 / l0 - lam * (a1 / l1)
        o_ref[s * bk:(s + 1) * bk, :] = _rms(o, nw_ref[...]) * (1.0 - DF_LAMBDA_INIT)


def diff_prompt(q, k, v, lam_pack, norm_w, batch, seq, bk=256, nsub=2):
    bq = bk * nsub
    nq = seq // bq
    return pl.pallas_call(
        functools.partial(_diff_prompt_kernel, bk=bk, nsub=nsub),
        grid=(batch, DF_HEADS, nq),
        in_specs=[pl.BlockSpec((bq, LANES), lambda b, h, i: (b * nq + i, h)),
                  pl.BlockSpec((seq, LANES), lambda b, h, i: (b, h)),
                  pl.BlockSpec((seq, LANES), lambda b, h, i: (b, h)),
                  pl.BlockSpec((8, DF_D), lambda b, h, i: (0, 0)),
                  pl.BlockSpec((1, DF_V), lambda b, h, i: (0, 0))],
        out_specs=pl.BlockSpec((bq, LANES), lambda b, h, i: (b * nq + i, h)),
        out_shape=jax.ShapeDtypeStruct((batch * seq, DF_HEADS * DF_V), F32),
        compiler_params=_params(("arbitrary", "arbitrary", "arbitrary")),
        name="diff_prompt",
    )(q, k, v, lam_pack, norm_w.reshape(1, DF_V))


def _diff_decode_kernel(pt_ref, q_ref, kn_ref, vn_ref, lam_ref, nw_ref, *refs, pp):
    k_refs = refs[:pp]
    v_refs = refs[pp:2 * pp]
    o_ref = refs[2 * pp]
    m_sc, l_sc, acc_sc = refs[2 * pp + 1:]
    p = pl.program_id(1)
    li = lax.broadcasted_iota(I32, (DF_V, 2 * DF_V), 0)
    ni = lax.broadcasted_iota(I32, (DF_V, 2 * DF_V), 1)
    ind = ((li // DF_D) == (ni // DF_V)).astype(BF16)
    q = q_ref[0] * (DF_D ** -0.5)

    @pl.when(p == 0)
    def _():
        z_own = _lane_group_sums((kn_ref[0] * q)[None], ind)[0]
        for c in range(2):
            m_sc[c] = z_own[:, c * DF_V:(c + 1) * DF_V]
            l_sc[c] = jnp.ones((DF_HEADS, DF_V), F32)
            acc_sc[c] = vn_ref[0]

    for i in range(pp):
        z = _lane_group_sums(k_refs[i][0] * q, ind)
        v = v_refs[i][0]
        for c in range(2):
            zc = z[:, :, c * DF_V:(c + 1) * DF_V]
            m_prev = m_sc[c]
            m_new = jnp.maximum(m_prev, jnp.max(zc, axis=0))
            pr = jnp.exp(zc - m_new)
            alpha = jnp.exp(m_prev - m_new)
            l_sc[c] = alpha * l_sc[c] + jnp.sum(pr, axis=0)
            acc_sc[c] = alpha * acc_sc[c] + jnp.sum(pr * v, axis=0)
            m_sc[c] = m_new

    @pl.when(p == pl.num_programs(1) - 1)
    def _():
        lam = _lambda_value(lam_ref)
        o = acc_sc[0] / l_sc[0] - lam * (acc_sc[1] / l_sc[1])
        o_ref[0] = _rms(o, nw_ref[...]) * (1.0 - DF_LAMBDA_INIT)


def diff_decode(q, k_new, v_new, cache_k, cache_v, page_table, lam_pack, norm_w, pp=4):
    b = q.shape[0]
    n_pages = page_table.shape[1]
    steps = n_pages // pp
    hd = (DF_HEADS, DF_V)

    def page_spec(i):
        return pl.BlockSpec((1, PAGE) + hd, lambda bi, p, pt: (pt[bi * n_pages + p * pp + i], 0, 0, 0))

    row = pl.BlockSpec((1,) + hd, lambda bi, p, pt: (bi, 0, 0))
    grid_spec = pltpu.PrefetchScalarGridSpec(
        num_scalar_prefetch=1,
        grid=(b, steps),
        in_specs=[row, row, row,
                  pl.BlockSpec((8, DF_D), lambda bi, p, pt: (0, 0)),
                  pl.BlockSpec((1, DF_V), lambda bi, p, pt: (0, 0))]
                 + [page_spec(i) for i in range(pp)] + [page_spec(i) for i in range(pp)],
        out_specs=row,
        scratch_shapes=[pltpu.VMEM((2,) + hd, F32), pltpu.VMEM((2,) + hd, F32), pltpu.VMEM((2,) + hd, F32)],
    )
    out = pl.pallas_call(
        functools.partial(_diff_decode_kernel, pp=pp),
        grid_spec=grid_spec,
        out_shape=jax.ShapeDtypeStruct((b,) + hd, F32),
        compiler_params=_params(("arbitrary", "arbitrary")),
        name="diff_decode",
    )(page_table.reshape(-1), q.reshape((b,) + hd), k_new.reshape((b,) + hd), v_new.reshape((b,) + hd),
      lam_pack, norm_w.reshape(1, DF_V), *([cache_k] * pp), *([cache_v] * pp))
    return out.reshape(b, DF_HEADS * DF_V)


IDX_NONE = 1e9


def _top_rows(tiles, ridx, k):
    n = tiles[0].shape[1]
    kio = lax.broadcasted_iota(I32, (k, n), 0)

    def body(i, carry):
        cur, vals, idxs = carry
        m = cur[0]
        for t in cur[1:]:
            m = jnp.maximum(m, t)
        m = jnp.max(m, axis=0, keepdims=True)
        f = None
        for t, ix in zip(cur, ridx):
            c = jnp.where(t == m, ix, IDX_NONE)
            f = c if f is None else jnp.minimum(f, c)
        f = jnp.min(f, axis=0, keepdims=True)
        cur = tuple(jnp.where(ix == f, NEG_INF, t) for t, ix in zip(cur, ridx))
        vals = jnp.where(kio == i, m, vals)
        idxs = jnp.where(kio == i, f, idxs)
        return cur, vals, idxs

    init = (tuple(tiles), jnp.zeros((k, n), F32), jnp.zeros((k, n), F32))
    _, vals, idxs = lax.fori_loop(0, k, body, init)
    return vals, idxs


def _pair_candidates(s1, s2):
    k, n = s1.shape
    sub = 8
    r8 = lax.broadcasted_iota(I32, (sub, n), 0)
    r8f = r8.astype(F32)
    tiles, ridx = [], []
    for a in range(k):
        nb = k // (a + 1)
        if nb == 1:
            break
        for b0 in range(0, nb, sub):
            t = s1[a:a + 1, :] + s2[b0:b0 + sub, :]
            ix = r8f + float(a * k + b0)
            if b0 + sub > nb:
                keep = r8 < (nb - b0)
                t = jnp.where(keep, t, NEG_INF)
                ix = jnp.where(keep, ix, IDX_NONE)
            tiles.append(t)
            ridx.append(ix)
    a0 = a
    assert (k - a0) % sub == 0
    for a1 in range(a0, k, sub):
        tiles.append(s1[a1:a1 + sub, :] + s2[0:1, :])
        ridx.append((r8f + float(a1)) * float(k))
    return tiles, ridx


def _pick_rows(table, sel):
    kk = table.shape[0]
    out = jnp.zeros(sel.shape, F32)
    for a in range(kk):
        out = jnp.where(sel == float(a), table[a:a + 1, :], out)
    return out


def _peer_select_kernel(h_ref, nw_ref, wq_ref, keys_ref, xn_ref, e_ref, g_ref, e_sc):
    kk = PEER_TOPK
    xn = _rms(h_ref[...], nw_ref[...])
    xn_ref[...] = xn
    q = jnp.dot(xn.astype(BF16), wq_ref[...], preferred_element_type=F32)
    tb = q.shape[0]
    key_idx = [lax.broadcasted_iota(I32, (PEER_NKEYS, tb), 0).astype(F32)]
    for h in range(PEER_HEADS):
        tops = []
        for c in range(2):
            j = h * 2 + c
            s_t = _bdot_nt(keys_ref[c], q[:, j * LANES:(j + 1) * LANES])
            tops.append(_top_rows([s_t], key_idx, kk))
        (s1, i1), (s2, i2) = tops
        c_top, c_idx = _top_rows(*_pair_candidates(s1, s2), kk)
        ia = jnp.floor(c_idx * (1.0 / kk))
        ib = c_idx - ia * kk
        e = _pick_rows(i1, ia) * float(PEER_NKEYS) + _pick_rows(i2, ib)
        ex = jnp.exp(c_top - jnp.max(c_top, axis=0, keepdims=True))
        g = ex / jnp.sum(ex, axis=0, keepdims=True)
        e_sc[h * kk:(h + 1) * kk, :] = e
        for blk in range(tb // LANES):
            g_ref[blk, h * kk:(h + 1) * kk, :] = g[:, blk * LANES:(blk + 1) * LANES]
    e_ref[...] = e_sc[...].T.astype(I32)


def peer_select(h, nw, wq_bf16, keys):
    t, d = h.shape
    tb = 2 * LANES if t % (2 * LANES) == 0 else LANES
    picks = PEER_HEADS * PEER_TOPK
    gb = tb // LANES
    return pl.pallas_call(
        _peer_select_kernel,
        grid=(t // tb,),
        in_specs=[pl.BlockSpec((tb, d), lambda i: (i, 0)),
                  pl.BlockSpec((1, d), lambda i: (0, 0)),
                  pl.BlockSpec(wq_bf16.shape, lambda i: (0, 0)),
                  pl.BlockSpec(keys.shape, lambda i: (0, 0, 0))],
        out_specs=[pl.BlockSpec((tb, d), lambda i: (i, 0)),
                   pl.BlockSpec((tb, picks), lambda i: (i, 0)),
                   pl.BlockSpec((gb, picks, LANES), lambda i: (i, 0, 0))],
        out_shape=[jax.ShapeDtypeStruct((t, d), F32),
                   jax.ShapeDtypeStruct((t, picks), I32),
                   jax.ShapeDtypeStruct((t // LANES, picks, LANES), F32)],
        scratch_shapes=[pltpu.VMEM((picks, tb), F32)],
        compiler_params=_params(("arbitrary",)),
        name="peer_select",
    )(h, nw.reshape(1, d), wq_bf16, keys)


def _peer_gather_kernel(e_hbm, uv_hbm, xn_ref, g_ref, h_ref, o_ref, ids_smem, rows, ids_sem, row_sem,
                        *, n_tok, n_slots):
    i = pl.program_id(0)
    nblk = pl.num_programs(0)
    picks = PEER_HEADS * PEER_TOPK
    d = xn_ref.shape[1]
    tb = xn_ref.shape[0]
    ahead = n_slots - 1
    assert n_tok % n_slots == 0 and n_tok > ahead
    cur = i % 2
    has_next = i + 1 < nblk

    def ids_copy(blk, buf):
        return pltpu.make_async_copy(e_hbm.at[pl.ds(blk * tb, tb)], ids_smem.at[buf], ids_sem.at[buf])

    def issue(buf, t, slot, j0, j1):
        for j in range(j0, j1):
            e = ids_smem[buf, t, j]
            pltpu.make_async_copy(uv_hbm.at[e], rows.at[slot, pl.ds(j, 1)],
                                  row_sem.at[slot]).start(priority=j % 2)

    def wait(slot):
        pltpu.make_async_copy(rows.at[slot], rows.at[slot], row_sem.at[slot]).wait()

    lane = lax.broadcasted_iota(I32, (picks, LANES), 1)
    half = picks // 2

    def token(t, slot, issue_half):
        wait(slot)
        x = xn_ref[pl.ds(t, 1), :]
        u = lax.bitcast_convert_type(rows[slot] & jnp.uint32(0xFFFF0000), F32)
        prod = u * x
        part = prod[:, 0:LANES]
        for c in range(1, d // LANES):
            part = part + prod[:, c * LANES:(c + 1) * LANES]
        issue_half(0, half)
        act = jnp.sum(part, axis=1, keepdims=True)
        g = jnp.sum(jnp.where(lane == t, g_ref[0], 0.0), axis=1, keepdims=True)
        wgt = g * jax.nn.gelu(act)
        v = lax.bitcast_convert_type(rows[slot] << 16, F32)
        mix = jnp.sum(wgt * v, axis=0, keepdims=True)
        issue_half(half, picks)
        o_ref[pl.ds(t, 1), :] = h_ref[pl.ds(t, 1), :] + mix

    @pl.when(i == 0)
    def _():
        first = ids_copy(0, 0)
        first.start()
        first.wait()
        for t in range(ahead):
            issue(0, t, t, 0, picks)

    @pl.when(has_next)
    def _():
        ids_copy(i + 1, 1 - cur).start()

    if n_tok < tb:
        o_ref[...] = h_ref[...]

    n_groups = n_tok // n_slots

    def group(gi, c):
        for u in range(n_slots):
            t = gi * n_slots + u
            token(t, u, functools.partial(issue, cur, t + ahead, (u + ahead) % n_slots))
        return c

    lax.fori_loop(0, n_groups - 1, group, 0)

    @pl.when(has_next)
    def _():
        ids_copy(i + 1, 1 - cur).wait()

    for u in range(n_slots):
        t = (n_groups - 1) * n_slots + u
        if u == 0:
            token(t, u, functools.partial(issue, cur, t + ahead, (u + ahead) % n_slots))
        else:
            def issue_next(j0, j1, u=u):
                @pl.when(has_next)
                def _():
                    issue(1 - cur, u - 1, (u + ahead) % n_slots, j0, j1)

            token(t, u, issue_next)


def pack_uv(u, v):
    ub = lax.bitcast_convert_type(u.astype(BF16), jnp.uint16).astype(jnp.uint32)
    vb = lax.bitcast_convert_type(v.astype(BF16), jnp.uint16).astype(jnp.uint32)
    return ((ub << 16) | vb).reshape(u.shape[0], 1, u.shape[1])


def peer_gather(e_ids, g_blocks, uv, xn, h, n_tok):
    t, d = h.shape
    tb = LANES
    nb = t // tb
    assert nb == 1 or n_tok == tb
    picks = PEER_HEADS * PEER_TOPK
    n_slots = 8
    return pl.pallas_call(
        functools.partial(_peer_gather_kernel, n_tok=n_tok, n_slots=n_slots),
        grid=(nb,),
        in_specs=[pl.BlockSpec(memory_space=pl.ANY),
                  pl.BlockSpec(memory_space=pl.ANY),
                  pl.BlockSpec((tb, d), lambda i: (i, 0)),
                  pl.BlockSpec((1, picks, tb), lambda i: (i, 0, 0)),
                  pl.BlockSpec((tb, d), lambda i: (i, 0))],
        out_specs=pl.BlockSpec((tb, d), lambda i: (i, 0)),
        out_shape=jax.ShapeDtypeStruct((t, d), F32),
        scratch_shapes=[pltpu.SMEM((2, tb, picks), I32),
                        pltpu.VMEM((n_slots, picks, d), jnp.uint32),
                        pltpu.SemaphoreType.DMA((2,)),
                        pltpu.SemaphoreType.DMA((n_slots,))],
        compiler_params=_params(("arbitrary",)),
        name="peer_gather",
    )(e_ids, uv, xn, g_blocks, h)


def peer_layer(h, nw, wq_bf16, keys, uv):
    t = h.shape[0]
    tp = -(-t // LANES) * LANES
    hp = h if tp == t else jnp.pad(h, ((0, tp - t), (0, 0)))
    xn, e_ids, g_blocks = peer_select(hp, nw, wq_bf16, keys)
    out = peer_gather(e_ids, g_blocks, uv, xn, hp, min(t, LANES))
    return out if tp == t else out[:t]


def kernel(x_prompt, x_sample, p_prompt, p_sample, state_mlstm_C, state_mlstm_n, state_mlstm_m, cache_sb_k, cache_sb_v, cache_diff_k, cache_diff_v, page_table, norm_mix0, w_in0, b_igate0, b_fgate0, mlstm_norm0, w_out0, norm_ffn0, peer_wq0, peer_keys0, peer_u0, peer_v0, ple_gate0, ple_proj0, norm_mix1, w_qkv1, lambda_q1, lambda_k1, lambda_q2, lambda_k2, diff_norm1, w_out1, norm_ffn1, peer_wq1, peer_keys1, peer_u1, peer_v1, ple_gate1, ple_proj1, final_norm):
    bp, sp, d = x_prompt.shape
    bs, ss, _ = x_sample.shape
    assert ss == 1
    ml_w = ML_HEADS * ML_DH
    sb_w = SB_HEADS * SB_DH
    df_w = DF_HEADS * DF_V

    g0 = 4 * ml_w
    g1 = g0 + 2 * ML_HEADS
    gates_w = jnp.pad(w_in0[:, g0:g1], ((0, 0), (0, LANES - 2 * ML_HEADS)))
    w_in = jnp.concatenate([w_in0[:, :g0], w_in0[:, g1:], gates_w], axis=1).astype(BF16)
    in_splits = (4 * ml_w, sb_w, sb_w, sb_w, LANES)
    gate_bias = jnp.pad(jnp.concatenate([b_igate0, b_fgate0]), (0, LANES - 2 * ML_HEADS)).reshape(1, LANES)
    w_out0_b = w_out0.astype(BF16)
    w_qkv = w_qkv1.astype(BF16)
    w_out1_b = w_out1.astype(BF16)
    lam_pack = jnp.pad(jnp.stack([lambda_q1, lambda_k1, lambda_q2, lambda_k2]), ((0, 4), (0, 0)))
    uv0 = pack_uv(peer_u0, peer_v0)
    uv1 = pack_uv(peer_u1, peer_v1)
    peer0 = (norm_ffn0, peer_wq0.astype(BF16), peer_keys0, uv0)
    peer1 = (norm_ffn1, peer_wq1.astype(BF16), peer_keys1, uv1)
    ple0 = (ple_gate0.astype(BF16), ple_proj0.astype(BF16))
    ple1 = (ple_gate1.astype(BF16), ple_proj1.astype(BF16))

    def layer0_tail(h, mix_acts, p0):
        h = proj_residual(h, mix_acts, [w_out0_b[:ml_w], w_out0_b[ml_w:]])
        h = peer_layer(h, *peer0)
        return ple(h, p0, *ple0, final_norm, False)

    def layer1_tail(h, o, p1):
        h = proj_residual(h, [o], [w_out1_b])
        h = peer_layer(h, *peer1)
        return ple(h, p1, *ple1, final_norm, True)

    tp = bp * sp
    h = x_prompt.reshape(tp, d)
    ml, sq, sk, sv, gates = norm_proj(h, norm_mix0, w_in, in_splits)
    h_ml, c_p, n_p, m_p = mlstm_prompt(ml, gates, gate_bias, mlstm_norm0, bp, sp)
    h_sb = sb_prompt(sq, sk, sv, bp, sp)
    h = layer0_tail(h, [h_ml, h_sb], p_prompt[0].reshape(tp, -1))
    q1, k1, v1 = norm_proj(h, norm_mix1, w_qkv, (df_w, df_w, df_w))
    o = diff_prompt(q1, k1, v1, lam_pack, diff_norm1, bp, sp)
    y_prompt = layer1_tail(h, o, p_prompt[1].reshape(tp, -1)).reshape(bp, sp, d)

    hs = x_sample.reshape(bs, d)
    ml_s, sq_s, sk_s, sv_s, gates_s = norm_proj(hs, norm_mix0, w_in, in_splits)
    h_ml_s, c_s, n_s, m_s = mlstm_step(ml_s, gates_s, gate_bias, mlstm_norm0,
                                       state_mlstm_C, state_mlstm_n, state_mlstm_m)
    h_sb_s = sb_decode(sq_s, cache_sb_k, cache_sb_v, page_table)
    hs = layer0_tail(hs, [h_ml_s, h_sb_s], p_sample[0].reshape(bs, -1))
    q1s, k1s, v1s = norm_proj(hs, norm_mix1, w_qkv, (df_w, df_w, df_w))
    o_s = diff_decode(q1s, k1s, v1s, cache_diff_k, cache_diff_v, page_table, lam_pack, diff_norm1)
    y_sample = layer1_tail(hs, o_s, p_sample[1].reshape(bs, -1)).reshape(bs, ss, d)

    return (y_prompt, y_sample,
            c_p, n_p, m_p.reshape(bp, ML_HEADS),
            sk.reshape(bp, sp, SB_HEADS, SB_DH), sv.reshape(bp, sp, SB_HEADS, SB_DH),
            k1.reshape(bp, sp, DF_HEADS, DF_V), v1.reshape(bp, sp, DF_HEADS, DF_V),
            c_s, n_s, m_s,
            sk_s.reshape(bs, ss, SB_HEADS, SB_DH), sv_s.reshape(bs, ss, SB_HEADS, SB_DH),
            k1s.reshape(bs, ss, DF_HEADS, DF_V), v1s.reshape(bs, ss, DF_HEADS, DF_V))
```

```python
import functools
import math

import jax
import jax.numpy as jnp
from jax import lax
from jax.experimental import pallas as pl
from jax.experimental.pallas import tpu as pltpu
from jax.experimental.pallas import tpu_sc as plsc

F32 = jnp.float32
BF16 = jnp.bfloat16
I32 = jnp.int32

RMS_EPS = 1e-6
LANES = 128
ML_HEADS, ML_DH, ML_CHUNK = 4, 128, 64
SB_HEADS, SB_DH = 8, 64
DF_HEADS, DF_D, DF_V = 8, 64, 128
DF_LAMBDA_INIT = 0.8 - 0.6 * math.exp(-0.3 * 1)
PEER_HEADS, PEER_NKEYS, PEER_TOPK = 8, 128, 16
PAGE = 128
VMEM_LIMIT = 56 * 1024 * 1024
NEG_INF = float("-inf")


def _params(sem):
    return pltpu.CompilerParams(dimension_semantics=sem, vmem_limit_bytes=VMEM_LIMIT)


def _bdot(a, b):
    return jnp.dot(a.astype(BF16), b.astype(BF16), preferred_element_type=F32)


def _bdot_nt(a, b):
    return lax.dot_general(a.astype(BF16), b.astype(BF16), (((1,), (1,)), ((), ())),
                           preferred_element_type=F32)


def _rms(x, w):
    return x * lax.rsqrt(jnp.mean(x * x, axis=-1, keepdims=True) + RMS_EPS) * w


def _row_tile(t):
    for tm in (256, 128, 64, 32, 16, 8):
        if t % tm == 0:
            return tm
    raise ValueError(f"row count {t} must be a multiple of 8")


def _norm_proj_kernel(x_ref, nw_ref, w_ref, *o_refs, splits):
    xn = _rms(x_ref[...], nw_ref[...]).astype(BF16)
    off = 0
    for o_ref, n in zip(o_refs, splits):
        o_ref[...] = jnp.dot(xn, w_ref[:, off:off + n], preferred_element_type=F32)
        off += n


def norm_proj(x, nw, w_bf16, splits):
    t, d = x.shape
    n = w_bf16.shape[1]
    assert sum(splits) == n and all(s % LANES == 0 for s in splits)
    tm = _row_tile(t)
    return pl.pallas_call(
        functools.partial(_norm_proj_kernel, splits=tuple(splits)),
        grid=(t // tm,),
        in_specs=[pl.BlockSpec((tm, d), lambda i: (i, 0)),
                  pl.BlockSpec((1, d), lambda i: (0, 0)),
                  pl.BlockSpec((d, n), lambda i: (0, 0))],
        out_specs=[pl.BlockSpec((tm, s), lambda i: (i, 0)) for s in splits],
        out_shape=[jax.ShapeDtypeStruct((t, s), F32) for s in splits],
        compiler_params=_params(("arbitrary",)),
        name="norm_proj",
    )(x, nw.reshape(1, d), w_bf16)


def _proj_res_kernel(*refs, n_in):
    res_ref = refs[0]
    a_refs = refs[1:1 + n_in]
    w_refs = refs[1 + n_in:1 + 2 * n_in]
    o_ref = refs[1 + 2 * n_in]
    acc = res_ref[...]
    for a_ref, w_ref in zip(a_refs, w_refs):
        acc = acc + jnp.dot(a_ref[...].astype(BF16), w_ref[...], preferred_element_type=F32)
    o_ref[...] = acc


def proj_residual(res, acts, ws_bf16):
    t, d = res.shape
    tm = _row_tile(t)
    n_in = len(acts)
    in_specs = [pl.BlockSpec((tm, d), lambda i: (i, 0))]
    in_specs += [pl.BlockSpec((tm, a.shape[1]), lambda i: (i, 0)) for a in acts]
    in_specs += [pl.BlockSpec(w.shape, lambda i: (0, 0)) for w in ws_bf16]
    return pl.pallas_call(
        functools.partial(_proj_res_kernel, n_in=n_in),
        grid=(t // tm,),
        in_specs=in_specs,
        out_specs=pl.BlockSpec((tm, d), lambda i: (i, 0)),
        out_shape=jax.ShapeDtypeStruct((t, d), F32),
        compiler_params=_params(("arbitrary",)),
        name="proj_residual",
    )(res, *acts, *ws_bf16)


def _ple_kernel(h_ref, p_ref, wg_ref, wp_ref, fn_ref, o_ref, *, final_norm):
    h = h_ref[...]
    gate = jax.nn.sigmoid(jnp.dot(h.astype(BF16), wg_ref[...], preferred_element_type=F32))
    emb = jnp.dot(p_ref[...].astype(BF16), wp_ref[...], preferred_element_type=F32)
    y = h + gate * emb
    if final_norm:
        y = _rms(y, fn_ref[...])
    o_ref[...] = y


def ple(h, p, wg_bf16, wp_bf16, fnorm, final_norm):
    t, d = h.shape
    pd = p.shape[1]
    tm = _row_tile(t)
    return pl.pallas_call(
        functools.partial(_ple_kernel, final_norm=final_norm),
        grid=(t // tm,),
        in_specs=[pl.BlockSpec((tm, d), lambda i: (i, 0)),
                  pl.BlockSpec((tm, pd), lambda i: (i, 0)),
                  pl.BlockSpec((d, d), lambda i: (0, 0)),
                  pl.BlockSpec((pd, d), lambda i: (0, 0)),
                  pl.BlockSpec((1, d), lambda i: (0, 0))],
        out_specs=pl.BlockSpec((tm, d), lambda i: (i, 0)),
        out_shape=jax.ShapeDtypeStruct((t, d), F32),
        compiler_params=_params(("arbitrary",)),
        name="ple",
    )(h, p, wg_bf16, wp_bf16, fnorm.reshape(1, d))


def _mlstm_chunk_kernel(q_ref, k_ref, v_ref, o_ref, g_ref, gb_ref, nw_ref,
                        h_ref, c_out, n_out, m_out, c_sc, n_sc, m_sc):
    ci = pl.program_id(1)
    L = ML_CHUNK

    @pl.when(ci == 0)
    def _():
        c_sc[...] = jnp.zeros_like(c_sc)
        n_sc[...] = jnp.zeros_like(n_sc)
        m_sc[...] = jnp.zeros_like(m_sc)

    hp = lax.Precision.HIGHEST
    g = g_ref[...] + gb_ref[...]
    lf = jax.nn.log_sigmoid(g)
    ri = lax.broadcasted_iota(I32, (L, L), 0)
    cj = lax.broadcasted_iota(I32, (L, L), 1)
    tril = (cj <= ri)
    low = tril.astype(F32)
    upp = (ri <= cj).astype(F32)
    eye = (lax.broadcasted_iota(I32, (LANES, LANES), 0) ==
           lax.broadcasted_iota(I32, (LANES, LANES), 1)).astype(F32)
    nt = (((1,), (1,)), ((), ()))
    g_t = lax.dot_general(eye, g, nt, precision=hp, preferred_element_type=F32)
    lf_t = lax.dot_general(eye, lf, nt, precision=hp, preferred_element_type=F32)
    b_c = jnp.dot(low, lf, precision=hp, preferred_element_type=F32)
    b_r = jnp.dot(lf_t, upp, precision=hp, preferred_element_type=F32)

    for h in range(ML_HEADS):
        sl = slice(h * ML_DH, (h + 1) * ML_DH)
        q = q_ref[:, sl] * (ML_DH ** -0.5)
        k = k_ref[:, sl]
        v = v_ref[:, sl]
        li_c = g[:, h:h + 1]
        li_r = g_t[h:h + 1, :]
        bc = b_c[:, ML_HEADS + h:ML_HEADS + h + 1]
        br = b_r[ML_HEADS + h:ML_HEADS + h + 1, :]
        m_prev = m_sc[:, h:h + 1]
        c_prev = c_sc[h]
        n_prev = n_sc[h:h + 1, :]
        dmat = jnp.where(tril, bc - br + li_r, NEG_INF)
        inter = bc + m_prev
        m_t = jnp.maximum(inter, jnp.max(dmat, axis=1, keepdims=True))
        w_inter = jnp.exp(inter - m_t)
        wqk = jnp.exp(dmat - m_t) * _bdot_nt(q, k)
        num = w_inter * _bdot_nt(q, c_prev) + _bdot(wqk, v)
        den = w_inter * jnp.sum(q * n_prev, axis=1, keepdims=True) + jnp.sum(wqk, axis=1, keepdims=True)
        hh = num / jnp.maximum(jnp.abs(den), jnp.exp(-m_t))
        b_last = br[:, L - 1:L]
        g_r = b_last - br + li_r
        g_c = b_last - bc + li_c
        m_new = jnp.maximum(b_last + m_prev, jnp.max(g_r, axis=1, keepdims=True))
        a = jnp.exp(b_last + m_prev - m_new)
        wg_c = jnp.exp(g_c - m_new)
        wv = (wg_c * v).astype(BF16)
        c_new = a * c_prev + lax.dot_general(wv, k.astype(BF16), (((0,), (0,)), ((), ())),
                                             preferred_element_type=F32)
        n_new = a * n_prev + jnp.sum(wg_c * k, axis=0, keepdims=True)
        c_sc[h] = c_new
        n_sc[h:h + 1, :] = n_new
        m_sc[:, h:h + 1] = m_new
        hn = _rms(hh, nw_ref[:, sl])
        h_ref[:, sl] = jax.nn.sigmoid(o_ref[:, sl]) * hn

    @pl.when(ci == pl.num_programs(1) - 1)
    def _():
        c_out[0] = c_sc[...]
        n_out[0] = n_sc[...]
        m_out[0] = m_sc[:, 0:ML_HEADS]


def mlstm_prompt(ml, gates, gate_bias, norm_w, batch, seq):
    L = ML_CHUNK
    nc = seq // L
    w = ML_HEADS * ML_DH
    col = lambda j: pl.BlockSpec((L, w), lambda b, c: (b * nc + c, j))
    return pl.pallas_call(
        _mlstm_chunk_kernel,
        grid=(batch, nc),
        in_specs=[col(0), col(1), col(2), col(3),
                  pl.BlockSpec((L, LANES), lambda b, c: (b * nc + c, 0)),
                  pl.BlockSpec((1, LANES), lambda b, c: (0, 0)),
                  pl.BlockSpec((1, w), lambda b, c: (0, 0))],
        out_specs=[pl.BlockSpec((L, w), lambda b, c: (b * nc + c, 0)),
                   pl.BlockSpec((1, ML_HEADS, ML_DH, ML_DH), lambda b, c: (b, 0, 0, 0)),
                   pl.BlockSpec((1, ML_HEADS, ML_DH), lambda b, c: (b, 0, 0)),
                   pl.BlockSpec((1, 1, ML_HEADS), lambda b, c: (b, 0, 0))],
        out_shape=[jax.ShapeDtypeStruct((batch * seq, w), F32),
                   jax.ShapeDtypeStruct((batch, ML_HEADS, ML_DH, ML_DH), F32),
                   jax.ShapeDtypeStruct((batch, ML_HEADS, ML_DH), F32),
                   jax.ShapeDtypeStruct((batch, 1, ML_HEADS), F32)],
        scratch_shapes=[pltpu.VMEM((ML_HEADS, ML_DH, ML_DH), F32),
                        pltpu.VMEM((ML_HEADS, ML_DH), F32),
                        pltpu.VMEM((1, LANES), F32)],
        compiler_params=_params(("arbitrary", "arbitrary")),
        name="mlstm_chunk",
    )(ml, ml, ml, ml, gates, gate_bias, norm_w.reshape(1, w))


def _mlstm_step_kernel(x_ref, g_ref, gb_ref, nw_ref, c_ref, n_ref, m_ref,
                       h_ref, c_out, n_out, m_out):
    w = ML_HEADS * ML_DH
    g = g_ref[0] + gb_ref[...]
    lf = jax.nn.log_sigmoid(g)
    eye = (lax.broadcasted_iota(I32, (ML_DH, ML_DH), 0) ==
           lax.broadcasted_iota(I32, (ML_DH, ML_DH), 1)).astype(F32)

    def to_col(row):
        return jnp.sum(eye * row, axis=1, keepdims=True)

    def to_row(colv):
        return jnp.sum(eye * colv, axis=0, keepdims=True)

    x = x_ref[0]
    for h in range(ML_HEADS):
        q = x[:, h * ML_DH:(h + 1) * ML_DH] * (ML_DH ** -0.5)
        k = x[:, w + h * ML_DH:w + (h + 1) * ML_DH]
        v = x[:, 2 * w + h * ML_DH:2 * w + (h + 1) * ML_DH]
        o = x[:, 3 * w + h * ML_DH:3 * w + (h + 1) * ML_DH]
        li = g[:, h:h + 1]
        lfh = lf[:, ML_HEADS + h:ML_HEADS + h + 1]
        m_prev = m_ref[0][:, h:h + 1]
        c_prev = c_ref[0, h]
        n_prev = n_ref[0][h:h + 1, :]
        inter = lfh + m_prev
        m_t = jnp.maximum(inter, li)
        w_inter = jnp.exp(inter - m_t)
        wqk = jnp.exp(li - m_t) * jnp.sum(q * k, axis=1, keepdims=True)
        v_col = to_col(v)
        num = w_inter * jnp.sum(c_prev * q, axis=1, keepdims=True) + wqk * v_col
        den = w_inter * jnp.sum(n_prev * q, axis=1, keepdims=True) + wqk
        hh = num / jnp.maximum(jnp.abs(den), jnp.exp(-m_t))
        m_new = m_t
        a = jnp.exp(inter - m_new)
        wg = jnp.exp(li - m_new)
        c_out[0, h] = a * c_prev + (wg * v_col) * k
        n_out[0, h:h + 1, :] = a * n_prev + wg * k
        m_out[0, :, h:h + 1] = m_new
        hr = to_row(hh)
        hn = _rms(hr, nw_ref[:, h * ML_DH:(h + 1) * ML_DH])
        h_ref[0, :, h * ML_DH:(h + 1) * ML_DH] = jax.nn.sigmoid(o) * hn


def mlstm_step(ml, gates, gate_bias, norm_w, c0, n0, m0):
    b = ml.shape[0]
    w = ML_HEADS * ML_DH
    row = lambda n: pl.BlockSpec((1, 1, n), lambda i: (i, 0, 0))
    h, c, n, m = pl.pallas_call(
        _mlstm_step_kernel,
        grid=(b,),
        in_specs=[row(4 * w), row(LANES),
                  pl.BlockSpec((1, LANES), lambda i: (0, 0)),
                  pl.BlockSpec((1, w), lambda i: (0, 0)),
                  pl.BlockSpec((1, ML_HEADS, ML_DH, ML_DH), lambda i: (i, 0, 0, 0)),
                  pl.BlockSpec((1, ML_HEADS, ML_DH), lambda i: (i, 0, 0)),
                  row(ML_HEADS)],
        out_specs=[row(w),
                   pl.BlockSpec((1, ML_HEADS, ML_DH, ML_DH), lambda i: (i, 0, 0, 0)),
                   pl.BlockSpec((1, ML_HEADS, ML_DH), lambda i: (i, 0, 0)),
                   row(ML_HEADS)],
        out_shape=[jax.ShapeDtypeStruct((b, 1, w), F32),
                   jax.ShapeDtypeStruct((b, ML_HEADS, ML_DH, ML_DH), F32),
                   jax.ShapeDtypeStruct((b, ML_HEADS, ML_DH), F32),
                   jax.ShapeDtypeStruct((b, 1, ML_HEADS), F32)],
        compiler_params=_params(("arbitrary",)),
        name="mlstm_step",
    )(ml.reshape(b, 1, 4 * w), gates.reshape(b, 1, LANES), gate_bias, norm_w.reshape(1, w),
      c0, n0, m0.reshape(b, 1, ML_HEADS))
    return h.reshape(b, w), c, n, m.reshape(b, ML_HEADS)


def _log_sig_pair(z):
    lg = jnp.log(1.0 + jnp.exp(-jnp.abs(z)))
    lb = jnp.minimum(z, 0.0) - lg
    return lb, lb - z


def _suffix_sum(l1m, tri_bf16):
    hi = l1m.astype(BF16)
    lo = (l1m - hi.astype(F32)).astype(BF16)
    return (jnp.dot(hi, tri_bf16, preferred_element_type=F32) +
            jnp.dot(lo, tri_bf16, preferred_element_type=F32))


def _wavefront(n_chains, stages):
    vals = [None] * n_chains
    for w in range(len(stages) + n_chains - 1):
        for c in range(n_chains):
            s = w - c
            if 0 <= s < len(stages):
                vals[c] = stages[s](c, vals[c])
    return vals


def _causal_schedule(nsub):
    return [(j, [(s, s == j) for s in range(j, nsub)]) for j in range(nsub - 1, -1, -1)]


def _sb_prompt_kernel(q_ref, k_ref, v_ref, tri_ref, o_ref, *, bk, nsub):
    base = pl.program_id(2) * nsub
    lane = lax.broadcasted_iota(I32, (bk, LANES), 1)
    tri = tri_ref[...]
    row = lax.broadcasted_iota(I32, (bk, bk), 0)
    colm = lax.broadcasted_iota(I32, (bk, bk), 1)
    causal = colm < row
    qops = []
    for s in range(nsub):
        q = q_ref[s * bk:(s + 1) * bk, :] * (SB_DH ** -0.5)
        qops.append([jnp.where((lane // SB_DH) == h, q, 0.0).astype(BF16) for h in range(2)])

    def run(kb, state, active):
        start = pl.multiple_of(kb * bk, bk)
        kblk = k_ref[pl.ds(start, bk), :].astype(BF16)
        vblk = v_ref[pl.ds(start, bk), :].astype(BF16)
        chains = [(s, h, m) for (s, m) in active for h in range(2)]

        def scores(c, _):
            s, h, _m = chains[c]
            return lax.dot_general(qops[s][h], kblk, (((1,), (1,)), ((), ())), preferred_element_type=F32)

        def logs(c, z):
            lb, l1m = _log_sig_pair(z)
            if chains[c][2]:
                l1m = jnp.where(causal, l1m, 0.0)
            hi = l1m.astype(BF16)
            return lb, l1m, hi, (l1m - hi.astype(F32)).astype(BF16)

        def suffix(c, v):
            lb, l1m, hi, lo = v
            rev = jnp.dot(hi, tri, preferred_element_type=F32) + jnp.dot(lo, tri, preferred_element_type=F32)
            return lb, l1m, rev

        def weights(c, v):
            s, h, m = chains[c]
            lb, l1m, rev = v
            tot = state[s * 2 + h][1]
            a = jnp.exp(lb + rev + tot)
            if m:
                a = jnp.where(causal, a, 0.0)
            return a.astype(BF16), tot + rev[:, 0:1] + l1m[:, 0:1]

        def values(c, v):
            s, h, _m = chains[c]
            a, tot = v
            return state[s * 2 + h][0] + jnp.dot(a, vblk, preferred_element_type=F32), tot

        outs = _wavefront(len(chains), [scores, logs, suffix, weights, values])
        new_state = list(state)
        for c, (s, h, _m) in enumerate(chains):
            new_state[s * 2 + h] = outs[c]
        return tuple(new_state)

    state = tuple((jnp.zeros((bk, LANES), F32), jnp.zeros((bk, 1), F32)) for _ in range(2 * nsub))
    for j, active in _causal_schedule(nsub):
        state = run(base + j, state, active)
    everyone = [(s, False) for s in range(nsub)]
    state = lax.fori_loop(0, base, lambda i, st: run(base - 1 - i, st, everyone), state)
    for s in range(nsub):
        o_ref[s * bk:(s + 1) * bk, :] = jnp.where(lane < SB_DH, state[2 * s][0], state[2 * s + 1][0])


def sb_prompt(sq, sk, sv, batch, seq, bk=256, nsub=2):
    bq = bk * nsub
    nq = seq // bq
    pairs = SB_HEADS // 2
    ri = lax.broadcasted_iota(I32, (bk, bk), 0)
    ci = lax.broadcasted_iota(I32, (bk, bk), 1)
    tri = (ri > ci).astype(BF16)
    return pl.pallas_call(
        functools.partial(_sb_prompt_kernel, bk=bk, nsub=nsub),
        grid=(batch, pairs, nq),
        in_specs=[pl.BlockSpec((bq, LANES), lambda b, p, i: (b * nq + i, p)),
                  pl.BlockSpec((seq, LANES), lambda b, p, i: (b, p)),
                  pl.BlockSpec((seq, LANES), lambda b, p, i: (b, p)),
                  pl.BlockSpec((bk, bk), lambda b, p, i: (0, 0))],
        out_specs=pl.BlockSpec((bq, LANES), lambda b, p, i: (b * nq + i, p)),
        out_shape=jax.ShapeDtypeStruct((batch * seq, SB_HEADS * SB_DH), F32),
        compiler_params=_params(("arbitrary", "arbitrary", "arbitrary")),
        name="sb_prompt",
    )(sq, sk, sv, tri)


def _lane_group_sums(x, ind_bf16):
    n, s, l = x.shape
    x2 = x.reshape(n * s, l)
    hi = x2.astype(BF16)
    lo = (x2 - hi.astype(F32)).astype(BF16)
    y = (jnp.dot(hi, ind_bf16, preferred_element_type=F32) +
         jnp.dot(lo, ind_bf16, preferred_element_type=F32))
    return y.reshape(n, s, ind_bf16.shape[1])


def _sb_decode_kernel(pt_ref, q_ref, *refs, pp):
    k_refs = refs[:pp]
    v_refs = refs[pp:2 * pp]
    tri_ref = refs[2 * pp]
    o_ref = refs[2 * pp + 1]
    acc_sc, tot_sc = refs[2 * pp + 2:]
    p = pl.program_id(1)

    @pl.when(p == 0)
    def _():
        acc_sc[...] = jnp.zeros_like(acc_sc)
        tot_sc[...] = jnp.zeros_like(tot_sc)

    tri = tri_ref[...]
    qcol = [q_ref[0, h] * (SB_DH ** -0.5) for h in range(SB_HEADS)]
    tot = tot_sc[...]
    for i in range(pp):
        z = jnp.concatenate([jnp.sum(k_refs[i][0, h] * qcol[h], axis=0, keepdims=True)
                             for h in range(SB_HEADS)], axis=0)
        lb, l1m = _log_sig_pair(z)
        rev = _suffix_sum(l1m, tri)
        a = jnp.exp(lb + rev + tot)
        tot = tot + rev[:, 0:1] + l1m[:, 0:1]
        for h in range(SB_HEADS):
            acc_sc[h] += v_refs[i][0, h] * a[h:h + 1, :]
    tot_sc[...] = tot

    @pl.when(p == pl.num_programs(1) - 1)
    def _():
        for h in range(SB_HEADS):
            o_ref[0, h] = jnp.sum(acc_sc[h], axis=1, keepdims=True)


def sb_decode(sq, cache_k, cache_v, page_table, pp=4):
    b = sq.shape[0]
    n_pages = page_table.shape[1]
    steps = n_pages // pp
    ck = jnp.transpose(cache_k, (0, 2, 3, 1))
    cv = jnp.transpose(cache_v, (0, 2, 3, 1))
    ri = lax.broadcasted_iota(I32, (PAGE, PAGE), 0)
    ci = lax.broadcasted_iota(I32, (PAGE, PAGE), 1)
    tri = (ri > ci).astype(BF16)

    def page_spec(i):
        return pl.BlockSpec((1, SB_HEADS, SB_DH, PAGE),
                            lambda bi, p, pt: (pt[bi * n_pages + (n_pages - 1 - (p * pp + i))], 0, 0, 0))

    col = pl.BlockSpec((1, SB_HEADS, SB_DH, 1), lambda bi, p, pt: (bi, 0, 0, 0))
    grid_spec = pltpu.PrefetchScalarGridSpec(
        num_scalar_prefetch=1,
        grid=(b, steps),
        in_specs=[col] + [page_spec(i) for i in range(pp)] + [page_spec(i) for i in range(pp)]
                 + [pl.BlockSpec((PAGE, PAGE), lambda bi, p, pt: (0, 0))],
        out_specs=col,
        scratch_shapes=[pltpu.VMEM((SB_HEADS, SB_DH, PAGE), F32), pltpu.VMEM((SB_HEADS, 1), F32)],
    )
    out = pl.pallas_call(
        functools.partial(_sb_decode_kernel, pp=pp),
        grid_spec=grid_spec,
        out_shape=jax.ShapeDtypeStruct((b, SB_HEADS, SB_DH, 1), F32),
        compiler_params=_params(("arbitrary", "arbitrary")),
        name="sb_decode",
    )(page_table.reshape(-1), sq.reshape(b, SB_HEADS, SB_DH, 1), *([ck] * pp), *([cv] * pp), tri)
    return out.reshape(b, SB_HEADS * SB_DH)


def _lambda_value(lam_ref):
    lp = lam_ref[...]
    s1 = jnp.sum(lp[0:1, :] * lp[1:2, :], axis=1, keepdims=True)
    s2 = jnp.sum(lp[2:3, :] * lp[3:4, :], axis=1, keepdims=True)
    return jnp.exp(s1) - jnp.exp(s2) + DF_LAMBDA_INIT


def _diff_prompt_kernel(q_ref, k_ref, v_ref, lam_ref, nw_ref, o_ref, *, bk, nsub):
    base = pl.program_id(2) * nsub
    lane = lax.broadcasted_iota(I32, (bk, LANES), 1)
    row = lax.broadcasted_iota(I32, (bk, bk), 0)
    colm = lax.broadcasted_iota(I32, (bk, bk), 1)
    causal = colm <= row
    qops = []
    for s in range(nsub):
        q = q_ref[s * bk:(s + 1) * bk, :] * (DF_D ** -0.5)
        qops.append([jnp.where((lane // DF_D) == c, q, 0.0).astype(BF16) for c in range(2)])

    def run(kb, state, active):
        start = pl.multiple_of(kb * bk, bk)
        kblk = k_ref[pl.ds(start, bk), :].astype(BF16)
        vblk = v_ref[pl.ds(start, bk), :].astype(BF16)
        chains = [(s, c, m) for (s, m) in active for c in range(2)]

        def scores(i, _):
            s, c, _m = chains[i]
            return lax.dot_general(qops[s][c], kblk, (((1,), (1,)), ((), ())), preferred_element_type=F32)

        def probs(i, z):
            s, c, m = chains[i]
            m_prev, l_prev, _ = state[s * 2 + c]
            if m:
                z = jnp.where(causal, z, NEG_INF)
            m_new = jnp.maximum(m_prev, jnp.max(z, axis=1, keepdims=True))
            p = jnp.exp(z - m_new)
            alpha = jnp.exp(m_prev - m_new)
            return m_new, alpha * l_prev + jnp.sum(p, axis=1, keepdims=True), alpha, p.astype(BF16)

        def values(i, v):
            m_new, l_new, alpha, p = v
            return m_new, l_new, alpha, jnp.dot(p, vblk, preferred_element_type=F32)

        def merge(i, v):
            s, c, _m = chains[i]
            m_new, l_new, alpha, pv = v
            return m_new, l_new, alpha * state[s * 2 + c][2] + pv

        outs = _wavefront(len(chains), [scores, probs, values, merge])
        new_state = list(state)
        for i, (s, c, _m) in enumerate(chains):
            new_state[s * 2 + c] = outs[i]
        return tuple(new_state)

    init = (jnp.full((bk, 1), NEG_INF, F32), jnp.zeros((bk, 1), F32), jnp.zeros((bk, LANES), F32))
    state = (init,) * (2 * nsub)
    for j, active in _causal_schedule(nsub):
        state = run(base + j, state, active)
    everyone = [(s, False) for s in range(nsub)]
    state = lax.fori_loop(0, base, lambda i, st: run(base - 1 - i, st, everyone), state)
    lam = _lambda_value(lam_ref)
    for s in range(nsub):
        (_, l0, a0), (_, l1, a1) = state[2 * s], state[2 * s + 1]
        o = a0 / l0 - lam * (a1 / l1)
        o_ref[s * bk:(s + 1) * bk, :] = _rms(o, nw_ref[...]) * (1.0 - DF_LAMBDA_INIT)


def diff_prompt(q, k, v, lam_pack, norm_w, batch, seq, bk=256, nsub=2):
    bq = bk * nsub
    nq = seq // bq
    return pl.pallas_call(
        functools.partial(_diff_prompt_kernel, bk=bk, nsub=nsub),
        grid=(batch, DF_HEADS, nq),
        in_specs=[pl.BlockSpec((bq, LANES), lambda b, h, i: (b * nq + i, h)),
                  pl.BlockSpec((seq, LANES), lambda b, h, i: (b, h)),
                  pl.BlockSpec((seq, LANES), lambda b, h, i: (b, h)),
                  pl.BlockSpec((8, DF_D), lambda b, h, i: (0, 0)),
                  pl.BlockSpec((1, DF_V), lambda b, h, i: (0, 0))],
        out_specs=pl.BlockSpec((bq, LANES), lambda b, h, i: (b * nq + i, h)),
        out_shape=jax.ShapeDtypeStruct((batch * seq, DF_HEADS * DF_V), F32),
        compiler_params=_params(("arbitrary", "arbitrary", "arbitrary")),
        name="diff_prompt",
    )(q, k, v, lam_pack, norm_w.reshape(1, DF_V))


def _diff_decode_kernel(pt_ref, q_ref, kn_ref, vn_ref, lam_ref, nw_ref, *refs, pp):
    k_refs = refs[:pp]
    v_refs = refs[pp:2 * pp]
    o_ref = refs[2 * pp]
    m_sc, l_sc, acc_sc = refs[2 * pp + 1:]
    p = pl.program_id(1)
    li = lax.broadcasted_iota(I32, (DF_V, 2 * DF_V), 0)
    ni = lax.broadcasted_iota(I32, (DF_V, 2 * DF_V), 1)
    ind = ((li // DF_D) == (ni // DF_V)).astype(BF16)
    q = q_ref[0] * (DF_D ** -0.5)

    @pl.when(p == 0)
    def _():
        z_own = _lane_group_sums((kn_ref[0] * q)[None], ind)[0]
        for c in range(2):
            m_sc[c] = z_own[:, c * DF_V:(c + 1) * DF_V]
            l_sc[c] = jnp.ones((DF_HEADS, DF_V), F32)
            acc_sc[c] = vn_ref[0]

    for i in range(pp):
        z = _lane_group_sums(k_refs[i][0] * q, ind)
        v = v_refs[i][0]
        for c in range(2):
            zc = z[:, :, c * DF_V:(c + 1) * DF_V]
            m_prev = m_sc[c]
            m_new = jnp.maximum(m_prev, jnp.max(zc, axis=0))
            pr = jnp.exp(zc - m_new)
            alpha = jnp.exp(m_prev - m_new)
            l_sc[c] = alpha * l_sc[c] + jnp.sum(pr, axis=0)
            acc_sc[c] = alpha * acc_sc[c] + jnp.sum(pr * v, axis=0)
            m_sc[c] = m_new

    @pl.when(p == pl.num_programs(1) - 1)
    def _():
        lam = _lambda_value(lam_ref)
        o = acc_sc[0] / l_sc[0] - lam * (acc_sc[1] / l_sc[1])
        o_ref[0] = _rms(o, nw_ref[...]) * (1.0 - DF_LAMBDA_INIT)


def diff_decode(q, k_new, v_new, cache_k, cache_v, page_table, lam_pack, norm_w, pp=4):
    b = q.shape[0]
    n_pages = page_table.shape[1]
    steps = n_pages // pp
    hd = (DF_HEADS, DF_V)

    def page_spec(i):
        return pl.BlockSpec((1, PAGE) + hd, lambda bi, p, pt: (pt[bi * n_pages + p * pp + i], 0, 0, 0))

    row = pl.BlockSpec((1,) + hd, lambda bi, p, pt: (bi, 0, 0))
    grid_spec = pltpu.PrefetchScalarGridSpec(
        num_scalar_prefetch=1,
        grid=(b, steps),
        in_specs=[row, row, row,
                  pl.BlockSpec((8, DF_D), lambda bi, p, pt: (0, 0)),
                  pl.BlockSpec((1, DF_V), lambda bi, p, pt: (0, 0))]
                 + [page_spec(i) for i in range(pp)] + [page_spec(i) for i in range(pp)],
        out_specs=row,
        scratch_shapes=[pltpu.VMEM((2,) + hd, F32), pltpu.VMEM((2,) + hd, F32), pltpu.VMEM((2,) + hd, F32)],
    )
    out = pl.pallas_call(
        functools.partial(_diff_decode_kernel, pp=pp),
        grid_spec=grid_spec,
        out_shape=jax.ShapeDtypeStruct((b,) + hd, F32),
        compiler_params=_params(("arbitrary", "arbitrary")),
        name="diff_decode",
    )(page_table.reshape(-1), q.reshape((b,) + hd), k_new.reshape((b,) + hd), v_new.reshape((b,) + hd),
      lam_pack, norm_w.reshape(1, DF_V), *([cache_k] * pp), *([cache_v] * pp))
    return out.reshape(b, DF_HEADS * DF_V)


IDX_NONE = 1e9
HI_HALF = -65536
SC_CORES, SC_SUBCORES = 2, 16
SC_CHUNK = 32


def _top_rows(tiles, ridx, k):
    n = tiles[0].shape[1]
    kio = lax.broadcasted_iota(I32, (k, n), 0)

    def body(i, carry):
        cur, vals, idxs = carry
        m = cur[0]
        for t in cur[1:]:
            m = jnp.maximum(m, t)
        m = jnp.max(m, axis=0, keepdims=True)
        f = None
        for t, ix in zip(cur, ridx):
            c = jnp.where(t == m, ix, IDX_NONE)
            f = c if f is None else jnp.minimum(f, c)
        f = jnp.min(f, axis=0, keepdims=True)
        cur = tuple(jnp.where(ix == f, NEG_INF, t) for t, ix in zip(cur, ridx))
        vals = jnp.where(kio == i, m, vals)
        idxs = jnp.where(kio == i, f, idxs)
        return cur, vals, idxs

    init = (tuple(tiles), jnp.zeros((k, n), F32), jnp.zeros((k, n), F32))
    _, vals, idxs = lax.fori_loop(0, k, body, init)
    return vals, idxs


def _pair_candidates(s1, s2):
    k, n = s1.shape
    sub = 8
    r8 = lax.broadcasted_iota(I32, (sub, n), 0)
    r8f = r8.astype(F32)
    tiles, ridx = [], []
    for a in range(k):
        nb = k // (a + 1)
        if nb == 1:
            break
        for b0 in range(0, nb, sub):
            t = s1[a:a + 1, :] + s2[b0:b0 + sub, :]
            ix = r8f + float(a * k + b0)
            if b0 + sub > nb:
                keep = r8 < (nb - b0)
                t = jnp.where(keep, t, NEG_INF)
                ix = jnp.where(keep, ix, IDX_NONE)
            tiles.append(t)
            ridx.append(ix)
    a0 = a
    assert (k - a0) % sub == 0
    for a1 in range(a0, k, sub):
        tiles.append(s1[a1:a1 + sub, :] + s2[0:1, :])
        ridx.append((r8f + float(a1)) * float(k))
    return tiles, ridx


def _pick_rows(table, sel):
    kk = table.shape[0]
    out = jnp.zeros(sel.shape, F32)
    for a in range(kk):
        out = jnp.where(sel == float(a), table[a:a + 1, :], out)
    return out


def _peer_select_kernel(h_ref, nw_ref, wq_ref, keys_ref, xn_ref, e_ref, g_ref, e_sc):
    kk = PEER_TOPK
    xn = _rms(h_ref[...], nw_ref[...])
    xn_ref[...] = xn
    q = jnp.dot(xn.astype(BF16), wq_ref[...], preferred_element_type=F32)
    tb = q.shape[0]
    key_idx = [lax.broadcasted_iota(I32, (PEER_NKEYS, tb), 0).astype(F32)]
    for h in range(PEER_HEADS):
        tops = []
        for c in range(2):
            j = h * 2 + c
            s_t = _bdot_nt(keys_ref[c], q[:, j * LANES:(j + 1) * LANES])
            tops.append(_top_rows([s_t], key_idx, kk))
        (s1, i1), (s2, i2) = tops
        c_top, c_idx = _top_rows(*_pair_candidates(s1, s2), kk)
        ia = jnp.floor(c_idx * (1.0 / kk))
        ib = c_idx - ia * kk
        e = _pick_rows(i1, ia) * float(PEER_NKEYS) + _pick_rows(i2, ib)
        ex = jnp.exp(c_top - jnp.max(c_top, axis=0, keepdims=True))
        g = ex / jnp.sum(ex, axis=0, keepdims=True)
        e_sc[h * kk:(h + 1) * kk, :] = e
        for blk in range(tb // LANES):
            g_ref[blk, h * kk:(h + 1) * kk, :] = g[:, blk * LANES:(blk + 1) * LANES]
    e_ref[...] = e_sc[...].T.astype(I32)


def peer_select(h, nw, wq_bf16, keys):
    t, d = h.shape
    tb = 2 * LANES if t % (2 * LANES) == 0 else LANES
    picks = PEER_HEADS * PEER_TOPK
    gb = tb // LANES
    return pl.pallas_call(
        _peer_select_kernel,
        grid=(t // tb,),
        in_specs=[pl.BlockSpec((tb, d), lambda i: (i, 0)),
                  pl.BlockSpec((1, d), lambda i: (0, 0)),
                  pl.BlockSpec(wq_bf16.shape, lambda i: (0, 0)),
                  pl.BlockSpec(keys.shape, lambda i: (0, 0, 0))],
        out_specs=[pl.BlockSpec((tb, d), lambda i: (i, 0)),
                   pl.BlockSpec((tb, picks), lambda i: (i, 0)),
                   pl.BlockSpec((gb, picks, LANES), lambda i: (i, 0, 0))],
        out_shape=[jax.ShapeDtypeStruct((t, d), F32),
                   jax.ShapeDtypeStruct((t, picks), I32),
                   jax.ShapeDtypeStruct((t // LANES, picks, LANES), F32)],
        scratch_shapes=[pltpu.VMEM((picks, tb), F32)],
        compiler_params=_params(("arbitrary",)),
        name="peer_select",
    )(h, nw.reshape(1, d), wq_bf16, keys)


def _peer_gather_kernel(e_hbm, uv_hbm, xn_ref, g_ref, h_ref, o_ref, ids_smem, rows, ids_sem, row_sem,
                        *, n_tok, n_slots):
    i = pl.program_id(0)
    nblk = pl.num_programs(0)
    picks = PEER_HEADS * PEER_TOPK
    d = xn_ref.shape[1]
    tb = xn_ref.shape[0]
    ahead = n_slots - 1
    assert n_tok % n_slots == 0 and n_tok > ahead
    cur = i % 2
    has_next = i + 1 < nblk

    def ids_copy(blk, buf):
        return pltpu.make_async_copy(e_hbm.at[pl.ds(blk * tb, tb)], ids_smem.at[buf], ids_sem.at[buf])

    def issue(buf, t, slot, j0, j1):
        for j in range(j0, j1):
            e = ids_smem[buf, t, j]
            pltpu.make_async_copy(uv_hbm.at[e], rows.at[slot, pl.ds(j, 1)],
                                  row_sem.at[slot]).start(priority=j % 2)

    def wait(slot):
        pltpu.make_async_copy(rows.at[slot], rows.at[slot], row_sem.at[slot]).wait()

    lane = lax.broadcasted_iota(I32, (picks, LANES), 1)
    half = picks // 2

    def token(t, slot, issue_half):
        wait(slot)
        x = xn_ref[pl.ds(t, 1), :]
        u = lax.bitcast_convert_type(rows[slot] & HI_HALF, F32)
        prod = u * x
        part = prod[:, 0:LANES]
        for c in range(1, d // LANES):
            part = part + prod[:, c * LANES:(c + 1) * LANES]
        issue_half(0, half)
        act = jnp.sum(part, axis=1, keepdims=True)
        g = jnp.sum(jnp.where(lane == t, g_ref[0], 0.0), axis=1, keepdims=True)
        wgt = g * jax.nn.gelu(act)
        v = lax.bitcast_convert_type(rows[slot] << 16, F32)
        mix = jnp.sum(wgt * v, axis=0, keepdims=True)
        issue_half(half, picks)
        o_ref[pl.ds(t, 1), :] = h_ref[pl.ds(t, 1), :] + mix

    @pl.when(i == 0)
    def _():
        first = ids_copy(0, 0)
        first.start()
        first.wait()
        for t in range(ahead):
            issue(0, t, t, 0, picks)

    @pl.when(has_next)
    def _():
        ids_copy(i + 1, 1 - cur).start()

    if n_tok < tb:
        o_ref[...] = h_ref[...]

    n_groups = n_tok // n_slots

    def group(gi, c):
        for u in range(n_slots):
            t = gi * n_slots + u
            token(t, u, functools.partial(issue, cur, t + ahead, (u + ahead) % n_slots))
        return c

    lax.fori_loop(0, n_groups - 1, group, 0)

    @pl.when(has_next)
    def _():
        ids_copy(i + 1, 1 - cur).wait()

    for u in range(n_slots):
        t = (n_groups - 1) * n_slots + u
        if u == 0:
            token(t, u, functools.partial(issue, cur, t + ahead, (u + ahead) % n_slots))
        else:
            def issue_next(j0, j1, u=u):
                @pl.when(has_next)
                def _():
                    issue(1 - cur, u - 1, (u + ahead) % n_slots, j0, j1)

            token(t, u, issue_next)


def pack_uv(u, v):
    ub = lax.bitcast_convert_type(u.astype(BF16), jnp.uint16).astype(jnp.uint32)
    vb = lax.bitcast_convert_type(v.astype(BF16), jnp.uint16).astype(jnp.uint32)
    return lax.bitcast_convert_type((ub << 16) | vb, I32)


def peer_gather(e_ids, g_blocks, uv, xn, h, n_tok, nb):
    t, d = h.shape
    tb = LANES
    assert nb == 1 or n_tok == tb
    picks = PEER_HEADS * PEER_TOPK
    n_slots = 8
    return pl.pallas_call(
        functools.partial(_peer_gather_kernel, n_tok=n_tok, n_slots=n_slots),
        grid=(nb,),
        in_specs=[pl.BlockSpec(memory_space=pl.ANY),
                  pl.BlockSpec(memory_space=pl.ANY),
                  pl.BlockSpec((tb, d), lambda i: (i, 0)),
                  pl.BlockSpec((1, picks, tb), lambda i: (i, 0, 0)),
                  pl.BlockSpec((tb, d), lambda i: (i, 0))],
        out_specs=pl.BlockSpec((tb, d), lambda i: (i, 0)),
        out_shape=jax.ShapeDtypeStruct((nb * tb, d), F32),
        scratch_shapes=[pltpu.SMEM((2, tb, picks), I32),
                        pltpu.VMEM((n_slots, picks, d), I32),
                        pltpu.SemaphoreType.DMA((2,)),
                        pltpu.SemaphoreType.DMA((n_slots,))],
        compiler_params=_params(("arbitrary",)),
        name="peer_gather",
    )(e_ids, uv, xn, g_blocks, h)


def _sc_gather_kernel(table_hbm, idx_hbm, out_hbm, idx0, idx1, rows0, rows1, sem0, sem1, *, per_worker):
    wid = lax.axis_index("s") * SC_CORES + lax.axis_index("c")
    base = wid * per_worker
    n_chunks = per_worker // SC_CHUNK
    bufs = ((idx0, rows0, sem0), (idx1, rows1, sem1))

    def gather(b):
        idx_v, rows_v, sem = bufs[b]
        return pltpu.make_async_copy(table_hbm.at[idx_v], rows_v, sem)

    def request(c, b):
        pltpu.sync_copy(idx_hbm.at[pl.ds(base + c * SC_CHUNK, SC_CHUNK)], bufs[b][0])
        gather(b).start()

    request(0, 0)

    def pair(k, carry):
        for b in range(2):
            c = 2 * k + b

            @pl.when(c + 1 < n_chunks)
            def _():
                request(c + 1, 1 - b)

            gather(b).wait()
            pltpu.sync_copy(bufs[b][1], out_hbm.at[pl.ds(base + c * SC_CHUNK, SC_CHUNK)])
        return carry

    lax.fori_loop(0, n_chunks // 2, pair, 0)


def sc_gather_rows(table, idx):
    n, = idx.shape
    d = table.shape[1]
    workers = SC_CORES * SC_SUBCORES
    per_worker = n // workers
    assert per_worker * workers == n and per_worker % (2 * SC_CHUNK) == 0
    mesh = plsc.VectorSubcoreMesh(core_axis_name="c", subcore_axis_name="s",
                                  num_cores=SC_CORES, num_subcores=SC_SUBCORES)
    run = pl.kernel(
        functools.partial(_sc_gather_kernel, per_worker=per_worker),
        out_type=jax.ShapeDtypeStruct((n, d), table.dtype),
        mesh=mesh,
        scratch_types=[pltpu.VMEM((SC_CHUNK,), I32), pltpu.VMEM((SC_CHUNK,), I32),
                       pltpu.VMEM((SC_CHUNK, d), table.dtype), pltpu.VMEM((SC_CHUNK, d), table.dtype),
                       pltpu.SemaphoreType.DMA, pltpu.SemaphoreType.DMA],
        name="sc_gather_rows",
    )
    return run(table, idx)


def _peer_mix_kernel(rows_ref, xn_ref, g_ref, h_ref, o_ref, *, tok0, tg):
    i = pl.program_id(0)
    picks = PEER_HEADS * PEER_TOPK
    d = xn_ref.shape[1]
    lane = lax.broadcasted_iota(I32, (picks, LANES), 1)
    g_tile = g_ref[0]
    for t in range(tg):
        words = rows_ref[t * picks:(t + 1) * picks, :]
        x = xn_ref[t:t + 1, :]
        prod = lax.bitcast_convert_type(words & HI_HALF, F32) * x
        part = prod[:, 0:LANES]
        for c in range(1, d // LANES):
            part = part + prod[:, c * LANES:(c + 1) * LANES]
        act = jnp.sum(part, axis=1, keepdims=True)
        tok_lane = (tok0 + i * tg + t) % LANES
        g = jnp.sum(jnp.where(lane == tok_lane, g_tile, 0.0), axis=1, keepdims=True)
        wgt = g * jax.nn.gelu(act)
        mix = jnp.sum(wgt * lax.bitcast_convert_type(words << 16, F32), axis=0, keepdims=True)
        o_ref[t:t + 1, :] = h_ref[t:t + 1, :] + mix


def peer_mix_rows(rows, g_blocks, xn, h, tok0):
    t, d = h.shape
    picks = PEER_HEADS * PEER_TOPK
    n = rows.shape[0] // picks
    tg = 8
    assert n % tg == 0 and tok0 % tg == 0 and LANES % tg == 0
    off = tok0 // tg
    return pl.pallas_call(
        functools.partial(_peer_mix_kernel, tok0=tok0, tg=tg),
        grid=(n // tg,),
        in_specs=[pl.BlockSpec((tg * picks, d), lambda i: (i, 0)),
                  pl.BlockSpec((tg, d), lambda i: (off + i, 0)),
                  pl.BlockSpec((1, picks, LANES), lambda i: ((tok0 + i * tg) // LANES, 0, 0)),
                  pl.BlockSpec((tg, d), lambda i: (off + i, 0))],
        out_specs=pl.BlockSpec((tg, d), lambda i: (i, 0)),
        out_shape=jax.ShapeDtypeStruct((n, d), F32),
        compiler_params=_params(("arbitrary",)),
        name="peer_mix_rows",
    )(rows, xn, g_blocks, h)


def _sc_share(t):
    return (t // 8) * 3 if t % 8192 == 0 else 0


def peer_layer(h, nw, wq_bf16, keys, uv):
    t, d = h.shape
    tp = -(-t // LANES) * LANES
    hp = h if tp == t else jnp.pad(h, ((0, tp - t), (0, 0)))
    xn, e_ids, g_blocks = peer_select(hp, nw, wq_bf16, keys)
    n_sc = _sc_share(tp)
    t_tc = tp - n_sc
    if n_sc:
        rows = sc_gather_rows(uv, e_ids[t_tc:].reshape(-1))
    out = peer_gather(e_ids, g_blocks, uv.reshape(-1, 1, d), xn, hp, min(t, LANES), t_tc // LANES)
    if n_sc:
        out = jnp.concatenate([out, peer_mix_rows(rows, g_blocks, xn, hp, t_tc)], axis=0)
    return out if tp == t else out[:t]


def kernel(x_prompt, x_sample, p_prompt, p_sample, state_mlstm_C, state_mlstm_n, state_mlstm_m, cache_sb_k, cache_sb_v, cache_diff_k, cache_diff_v, page_table, norm_mix0, w_in0, b_igate0, b_fgate0, mlstm_norm0, w_out0, norm_ffn0, peer_wq0, peer_keys0, peer_u0, peer_v0, ple_gate0, ple_proj0, norm_mix1, w_qkv1, lambda_q1, lambda_k1, lambda_q2, lambda_k2, diff_norm1, w_out1, norm_ffn1, peer_wq1, peer_keys1, peer_u1, peer_v1, ple_gate1, ple_proj1, final_norm):
    bp, sp, d = x_prompt.shape
    bs, ss, _ = x_sample.shape
    assert ss == 1
    ml_w = ML_HEADS * ML_DH
    sb_w = SB_HEADS * SB_DH
    df_w = DF_HEADS * DF_V

    g0 = 4 * ml_w
    g1 = g0 + 2 * ML_HEADS
    gates_w = jnp.pad(w_in0[:, g0:g1], ((0, 0), (0, LANES - 2 * ML_HEADS)))
    w_in = jnp.concatenate([w_in0[:, :g0], w_in0[:, g1:], gates_w], axis=1).astype(BF16)
    in_splits = (4 * ml_w, sb_w, sb_w, sb_w, LANES)
    gate_bias = jnp.pad(jnp.concatenate([b_igate0, b_fgate0]), (0, LANES - 2 * ML_HEADS)).reshape(1, LANES)
    w_out0_b = w_out0.astype(BF16)
    w_qkv = w_qkv1.astype(BF16)
    w_out1_b = w_out1.astype(BF16)
    lam_pack = jnp.pad(jnp.stack([lambda_q1, lambda_k1, lambda_q2, lambda_k2]), ((0, 4), (0, 0)))
    uv0 = pack_uv(peer_u0, peer_v0)
    uv1 = pack_uv(peer_u1, peer_v1)
    peer0 = (norm_ffn0, peer_wq0.astype(BF16), peer_keys0, uv0)
    peer1 = (norm_ffn1, peer_wq1.astype(BF16), peer_keys1, uv1)
    ple0 = (ple_gate0.astype(BF16), ple_proj0.astype(BF16))
    ple1 = (ple_gate1.astype(BF16), ple_proj1.astype(BF16))

    def layer0_tail(h, mix_acts, p0):
        h = proj_residual(h, mix_acts, [w_out0_b[:ml_w], w_out0_b[ml_w:]])
        h = peer_layer(h, *peer0)
        return ple(h, p0, *ple0, final_norm, False)

    def layer1_tail(h, o, p1):
        h = proj_residual(h, [o], [w_out1_b])
        h = peer_layer(h, *peer1)
        return ple(h, p1, *ple1, final_norm, True)

    tp = bp * sp
    h = x_prompt.reshape(tp, d)
    ml, sq, sk, sv, gates = norm_proj(h, norm_mix0, w_in, in_splits)
    h_ml, c_p, n_p, m_p = mlstm_prompt(ml, gates, gate_bias, mlstm_norm0, bp, sp)
    h_sb = sb_prompt(sq, sk, sv, bp, sp)
    h = layer0_tail(h, [h_ml, h_sb], p_prompt[0].reshape(tp, -1))
    q1, k1, v1 = norm_proj(h, norm_mix1, w_qkv, (df_w, df_w, df_w))
    o = diff_prompt(q1, k1, v1, lam_pack, diff_norm1, bp, sp)
    y_prompt = layer1_tail(h, o, p_prompt[1].reshape(tp, -1)).reshape(bp, sp, d)

    hs = x_sample.reshape(bs, d)
    ml_s, sq_s, sk_s, sv_s, gates_s = norm_proj(hs, norm_mix0, w_in, in_splits)
    h_ml_s, c_s, n_s, m_s = mlstm_step(ml_s, gates_s, gate_bias, mlstm_norm0,
                                       state_mlstm_C, state_mlstm_n, state_mlstm_m)
    h_sb_s = sb_decode(sq_s, cache_sb_k, cache_sb_v, page_table)
    hs = layer0_tail(hs, [h_ml_s, h_sb_s], p_sample[0].reshape(bs, -1))
    q1s, k1s, v1s = norm_proj(hs, norm_mix1, w_qkv, (df_w, df_w, df_w))
    o_s = diff_decode(q1s, k1s, v1s, cache_diff_k, cache_diff_v, page_table, lam_pack, diff_norm1)
    y_sample = layer1_tail(hs, o_s, p_sample[1].reshape(bs, -1)).reshape(bs, ss, d)

    return (y_prompt, y_sample,
            c_p, n_p, m_p.reshape(bp, ML_HEADS),
            sk.reshape(bp, sp, SB_HEADS, SB_DH), sv.reshape(bp, sp, SB_HEADS, SB_DH),
            k1.reshape(bp, sp, DF_HEADS, DF_V), v1.reshape(bp, sp, DF_HEADS, DF_V),
            c_s, n_s, m_s,
            sk_s.reshape(bs, ss, SB_HEADS, SB_DH), sv_s.reshape(bs, ss, SB_HEADS, SB_DH),
            k1s.reshape(bs, ss, DF_HEADS, DF_V), v1s.reshape(bs, ss, DF_HEADS, DF_V))
```

```python
import functools
import math

import jax
import jax.numpy as jnp
from jax import lax
from jax.experimental import pallas as pl
from jax.experimental.pallas import tpu as pltpu
from jax.experimental.pallas import tpu_sc as plsc

F32 = jnp.float32
BF16 = jnp.bfloat16
I32 = jnp.int32

RMS_EPS = 1e-6
LANES = 128
ML_HEADS, ML_DH, ML_CHUNK = 4, 128, 64
SB_HEADS, SB_DH = 8, 64
DF_HEADS, DF_D, DF_V = 8, 64, 128
DF_LAMBDA_INIT = 0.8 - 0.6 * math.exp(-0.3 * 1)
PEER_HEADS, PEER_NKEYS, PEER_TOPK = 8, 128, 16
PAGE = 128
VMEM_LIMIT = 56 * 1024 * 1024
NEG_INF = float("-inf")


def _params(sem):
    return pltpu.CompilerParams(dimension_semantics=sem, vmem_limit_bytes=VMEM_LIMIT)


def _bdot(a, b):
    return jnp.dot(a.astype(BF16), b.astype(BF16), preferred_element_type=F32)


def _bdot_nt(a, b):
    return lax.dot_general(a.astype(BF16), b.astype(BF16), (((1,), (1,)), ((), ())),
                           preferred_element_type=F32)


def _rms(x, w):
    return x * lax.rsqrt(jnp.mean(x * x, axis=-1, keepdims=True) + RMS_EPS) * w


def _row_tile(t):
    for tm in (256, 128, 64, 32, 16, 8):
        if t % tm == 0:
            return tm
    raise ValueError(f"row count {t} must be a multiple of 8")


def _norm_proj_kernel(x_ref, nw_ref, w_ref, *o_refs, splits):
    xn = _rms(x_ref[...], nw_ref[...]).astype(BF16)
    off = 0
    for o_ref, n in zip(o_refs, splits):
        o_ref[...] = jnp.dot(xn, w_ref[:, off:off + n], preferred_element_type=F32)
        off += n


def norm_proj(x, nw, w_bf16, splits):
    t, d = x.shape
    n = w_bf16.shape[1]
    assert sum(splits) == n and all(s % LANES == 0 for s in splits)
    tm = _row_tile(t)
    return pl.pallas_call(
        functools.partial(_norm_proj_kernel, splits=tuple(splits)),
        grid=(t // tm,),
        in_specs=[pl.BlockSpec((tm, d), lambda i: (i, 0)),
                  pl.BlockSpec((1, d), lambda i: (0, 0)),
                  pl.BlockSpec((d, n), lambda i: (0, 0))],
        out_specs=[pl.BlockSpec((tm, s), lambda i: (i, 0)) for s in splits],
        out_shape=[jax.ShapeDtypeStruct((t, s), F32) for s in splits],
        compiler_params=_params(("arbitrary",)),
        name="norm_proj",
    )(x, nw.reshape(1, d), w_bf16)


def _proj_res_kernel(*refs, n_in):
    res_ref = refs[0]
    a_refs = refs[1:1 + n_in]
    w_refs = refs[1 + n_in:1 + 2 * n_in]
    o_ref = refs[1 + 2 * n_in]
    acc = res_ref[...]
    for a_ref, w_ref in zip(a_refs, w_refs):
        acc = acc + jnp.dot(a_ref[...].astype(BF16), w_ref[...], preferred_element_type=F32)
    o_ref[...] = acc


def proj_residual(res, acts, ws_bf16):
    t, d = res.shape
    tm = _row_tile(t)
    n_in = len(acts)
    in_specs = [pl.BlockSpec((tm, d), lambda i: (i, 0))]
    in_specs += [pl.BlockSpec((tm, a.shape[1]), lambda i: (i, 0)) for a in acts]
    in_specs += [pl.BlockSpec(w.shape, lambda i: (0, 0)) for w in ws_bf16]
    return pl.pallas_call(
        functools.partial(_proj_res_kernel, n_in=n_in),
        grid=(t // tm,),
        in_specs=in_specs,
        out_specs=pl.BlockSpec((tm, d), lambda i: (i, 0)),
        out_shape=jax.ShapeDtypeStruct((t, d), F32),
        compiler_params=_params(("arbitrary",)),
        name="proj_residual",
    )(res, *acts, *ws_bf16)


def _ple_kernel(h_ref, p_ref, wg_ref, wp_ref, fn_ref, o_ref, *, final_norm):
    h = h_ref[...]
    gate = jax.nn.sigmoid(jnp.dot(h.astype(BF16), wg_ref[...], preferred_element_type=F32))
    emb = jnp.dot(p_ref[...].astype(BF16), wp_ref[...], preferred_element_type=F32)
    y = h + gate * emb
    if final_norm:
        y = _rms(y, fn_ref[...])
    o_ref[...] = y


def ple(h, p, wg_bf16, wp_bf16, fnorm, final_norm):
    t, d = h.shape
    pd = p.shape[1]
    tm = _row_tile(t)
    return pl.pallas_call(
        functools.partial(_ple_kernel, final_norm=final_norm),
        grid=(t // tm,),
        in_specs=[pl.BlockSpec((tm, d), lambda i: (i, 0)),
                  pl.BlockSpec((tm, pd), lambda i: (i, 0)),
                  pl.BlockSpec((d, d), lambda i: (0, 0)),
                  pl.BlockSpec((pd, d), lambda i: (0, 0)),
                  pl.BlockSpec((1, d), lambda i: (0, 0))],
        out_specs=pl.BlockSpec((tm, d), lambda i: (i, 0)),
        out_shape=jax.ShapeDtypeStruct((t, d), F32),
        compiler_params=_params(("arbitrary",)),
        name="ple",
    )(h, p, wg_bf16, wp_bf16, fnorm.reshape(1, d))


def _mlstm_chunk_kernel(q_ref, k_ref, v_ref, o_ref, g_ref, gb_ref, nw_ref,
                        h_ref, c_out, n_out, m_out, c_sc, n_sc, m_sc):
    ci = pl.program_id(1)
    L = ML_CHUNK

    @pl.when(ci == 0)
    def _():
        c_sc[...] = jnp.zeros_like(c_sc)
        n_sc[...] = jnp.zeros_like(n_sc)
        m_sc[...] = jnp.zeros_like(m_sc)

    hp = lax.Precision.HIGHEST
    g = g_ref[...] + gb_ref[...]
    lf = jax.nn.log_sigmoid(g)
    ri = lax.broadcasted_iota(I32, (L, L), 0)
    cj = lax.broadcasted_iota(I32, (L, L), 1)
    tril = (cj <= ri)
    low = tril.astype(F32)
    upp = (ri <= cj).astype(F32)
    eye = (lax.broadcasted_iota(I32, (LANES, LANES), 0) ==
           lax.broadcasted_iota(I32, (LANES, LANES), 1)).astype(F32)
    nt = (((1,), (1,)), ((), ()))
    g_t = lax.dot_general(eye, g, nt, precision=hp, preferred_element_type=F32)
    lf_t = lax.dot_general(eye, lf, nt, precision=hp, preferred_element_type=F32)
    b_c = jnp.dot(low, lf, precision=hp, preferred_element_type=F32)
    b_r = jnp.dot(lf_t, upp, precision=hp, preferred_element_type=F32)

    for h in range(ML_HEADS):
        sl = slice(h * ML_DH, (h + 1) * ML_DH)
        q = q_ref[:, sl] * (ML_DH ** -0.5)
        k = k_ref[:, sl]
        v = v_ref[:, sl]
        li_c = g[:, h:h + 1]
        li_r = g_t[h:h + 1, :]
        bc = b_c[:, ML_HEADS + h:ML_HEADS + h + 1]
        br = b_r[ML_HEADS + h:ML_HEADS + h + 1, :]
        m_prev = m_sc[:, h:h + 1]
        c_prev = c_sc[h]
        n_prev = n_sc[h:h + 1, :]
        dmat = jnp.where(tril, bc - br + li_r, NEG_INF)
        inter = bc + m_prev
        m_t = jnp.maximum(inter, jnp.max(dmat, axis=1, keepdims=True))
        w_inter = jnp.exp(inter - m_t)
        wqk = jnp.exp(dmat - m_t) * _bdot_nt(q, k)
        num = w_inter * _bdot_nt(q, c_prev) + _bdot(wqk, v)
        den = w_inter * jnp.sum(q * n_prev, axis=1, keepdims=True) + jnp.sum(wqk, axis=1, keepdims=True)
        hh = num / jnp.maximum(jnp.abs(den), jnp.exp(-m_t))
        b_last = br[:, L - 1:L]
        g_r = b_last - br + li_r
        g_c = b_last - bc + li_c
        m_new = jnp.maximum(b_last + m_prev, jnp.max(g_r, axis=1, keepdims=True))
        a = jnp.exp(b_last + m_prev - m_new)
        wg_c = jnp.exp(g_c - m_new)
        wv = (wg_c * v).astype(BF16)
        c_new = a * c_prev + lax.dot_general(wv, k.astype(BF16), (((0,), (0,)), ((), ())),
                                             preferred_element_type=F32)
        n_new = a * n_prev + jnp.sum(wg_c * k, axis=0, keepdims=True)
        c_sc[h] = c_new
        n_sc[h:h + 1, :] = n_new
        m_sc[:, h:h + 1] = m_new
        hn = _rms(hh, nw_ref[:, sl])
        h_ref[:, sl] = jax.nn.sigmoid(o_ref[:, sl]) * hn

    @pl.when(ci == pl.num_programs(1) - 1)
    def _():
        c_out[0] = c_sc[...]
        n_out[0] = n_sc[...]
        m_out[0] = m_sc[:, 0:ML_HEADS]


def mlstm_prompt(ml, gates, gate_bias, norm_w, batch, seq):
    L = ML_CHUNK
    nc = seq // L
    w = ML_HEADS * ML_DH
    col = lambda j: pl.BlockSpec((L, w), lambda b, c: (b * nc + c, j))
    return pl.pallas_call(
        _mlstm_chunk_kernel,
        grid=(batch, nc),
        in_specs=[col(0), col(1), col(2), col(3),
                  pl.BlockSpec((L, LANES), lambda b, c: (b * nc + c, 0)),
                  pl.BlockSpec((1, LANES), lambda b, c: (0, 0)),
                  pl.BlockSpec((1, w), lambda b, c: (0, 0))],
        out_specs=[pl.BlockSpec((L, w), lambda b, c: (b * nc + c, 0)),
                   pl.BlockSpec((1, ML_HEADS, ML_DH, ML_DH), lambda b, c: (b, 0, 0, 0)),
                   pl.BlockSpec((1, ML_HEADS, ML_DH), lambda b, c: (b, 0, 0)),
                   pl.BlockSpec((1, 1, ML_HEADS), lambda b, c: (b, 0, 0))],
        out_shape=[jax.ShapeDtypeStruct((batch * seq, w), F32),
                   jax.ShapeDtypeStruct((batch, ML_HEADS, ML_DH, ML_DH), F32),
                   jax.ShapeDtypeStruct((batch, ML_HEADS, ML_DH), F32),
                   jax.ShapeDtypeStruct((batch, 1, ML_HEADS), F32)],
        scratch_shapes=[pltpu.VMEM((ML_HEADS, ML_DH, ML_DH), F32),
                        pltpu.VMEM((ML_HEADS, ML_DH), F32),
                        pltpu.VMEM((1, LANES), F32)],
        compiler_params=_params(("arbitrary", "arbitrary")),
        name="mlstm_chunk",
    )(ml, ml, ml, ml, gates, gate_bias, norm_w.reshape(1, w))


def _mlstm_step_kernel(x_ref, g_ref, gb_ref, nw_ref, c_ref, n_ref, m_ref,
                       h_ref, c_out, n_out, m_out):
    w = ML_HEADS * ML_DH
    g = g_ref[0] + gb_ref[...]
    lf = jax.nn.log_sigmoid(g)
    eye = (lax.broadcasted_iota(I32, (ML_DH, ML_DH), 0) ==
           lax.broadcasted_iota(I32, (ML_DH, ML_DH), 1)).astype(F32)

    def to_col(row):
        return jnp.sum(eye * row, axis=1, keepdims=True)

    def to_row(colv):
        return jnp.sum(eye * colv, axis=0, keepdims=True)

    x = x_ref[0]
    for h in range(ML_HEADS):
        q = x[:, h * ML_DH:(h + 1) * ML_DH] * (ML_DH ** -0.5)
        k = x[:, w + h * ML_DH:w + (h + 1) * ML_DH]
        v = x[:, 2 * w + h * ML_DH:2 * w + (h + 1) * ML_DH]
        o = x[:, 3 * w + h * ML_DH:3 * w + (h + 1) * ML_DH]
        li = g[:, h:h + 1]
        lfh = lf[:, ML_HEADS + h:ML_HEADS + h + 1]
        m_prev = m_ref[0][:, h:h + 1]
        c_prev = c_ref[0, h]
        n_prev = n_ref[0][h:h + 1, :]
        inter = lfh + m_prev
        m_t = jnp.maximum(inter, li)
        w_inter = jnp.exp(inter - m_t)
        wqk = jnp.exp(li - m_t) * jnp.sum(q * k, axis=1, keepdims=True)
        v_col = to_col(v)
        num = w_inter * jnp.sum(c_prev * q, axis=1, keepdims=True) + wqk * v_col
        den = w_inter * jnp.sum(n_prev * q, axis=1, keepdims=True) + wqk
        hh = num / jnp.maximum(jnp.abs(den), jnp.exp(-m_t))
        m_new = m_t
        a = jnp.exp(inter - m_new)
        wg = jnp.exp(li - m_new)
        c_out[0, h] = a * c_prev + (wg * v_col) * k
        n_out[0, h:h + 1, :] = a * n_prev + wg * k
        m_out[0, :, h:h + 1] = m_new
        hr = to_row(hh)
        hn = _rms(hr, nw_ref[:, h * ML_DH:(h + 1) * ML_DH])
        h_ref[0, :, h * ML_DH:(h + 1) * ML_DH] = jax.nn.sigmoid(o) * hn


def mlstm_step(ml, gates, gate_bias, norm_w, c0, n0, m0):
    b = ml.shape[0]
    w = ML_HEADS * ML_DH
    row = lambda n: pl.BlockSpec((1, 1, n), lambda i: (i, 0, 0))
    h, c, n, m = pl.pallas_call(
        _mlstm_step_kernel,
        grid=(b,),
        in_specs=[row(4 * w), row(LANES),
                  pl.BlockSpec((1, LANES), lambda i: (0, 0)),
                  pl.BlockSpec((1, w), lambda i: (0, 0)),
                  pl.BlockSpec((1, ML_HEADS, ML_DH, ML_DH), lambda i: (i, 0, 0, 0)),
                  pl.BlockSpec((1, ML_HEADS, ML_DH), lambda i: (i, 0, 0)),
                  row(ML_HEADS)],
        out_specs=[row(w),
                   pl.BlockSpec((1, ML_HEADS, ML_DH, ML_DH), lambda i: (i, 0, 0, 0)),
                   pl.BlockSpec((1, ML_HEADS, ML_DH), lambda i: (i, 0, 0)),
                   row(ML_HEADS)],
        out_shape=[jax.ShapeDtypeStruct((b, 1, w), F32),
                   jax.ShapeDtypeStruct((b, ML_HEADS, ML_DH, ML_DH), F32),
                   jax.ShapeDtypeStruct((b, ML_HEADS, ML_DH), F32),
                   jax.ShapeDtypeStruct((b, 1, ML_HEADS), F32)],
        compiler_params=_params(("arbitrary",)),
        name="mlstm_step",
    )(ml.reshape(b, 1, 4 * w), gates.reshape(b, 1, LANES), gate_bias, norm_w.reshape(1, w),
      c0, n0, m0.reshape(b, 1, ML_HEADS))
    return h.reshape(b, w), c, n, m.reshape(b, ML_HEADS)


def _log_sig_pair(z):
    lg = jnp.log(1.0 + jnp.exp(-jnp.abs(z)))
    lb = jnp.minimum(z, 0.0) - lg
    return lb, lb - z


def _suffix_sum(l1m, tri_bf16):
    hi = l1m.astype(BF16)
    lo = (l1m - hi.astype(F32)).astype(BF16)
    return (jnp.dot(hi, tri_bf16, preferred_element_type=F32) +
            jnp.dot(lo, tri_bf16, preferred_element_type=F32))


def _wavefront(n_chains, stages):
    vals = [None] * n_chains
    for w in range(len(stages) + n_chains - 1):
        for c in range(n_chains):
            s = w - c
            if 0 <= s < len(stages):
                vals[c] = stages[s](c, vals[c])
    return vals


def _causal_schedule(nsub):
    return [(j, [(s, s == j) for s in range(j, nsub)]) for j in range(nsub - 1, -1, -1)]


def _sb_prompt_kernel(q_ref, k_ref, v_ref, tri_ref, o_ref, *, bk, nsub):
    base = pl.program_id(2) * nsub
    lane = lax.broadcasted_iota(I32, (bk, LANES), 1)
    tri = tri_ref[...]
    row = lax.broadcasted_iota(I32, (bk, bk), 0)
    colm = lax.broadcasted_iota(I32, (bk, bk), 1)
    causal = colm < row
    qops = []
    for s in range(nsub):
        q = q_ref[s * bk:(s + 1) * bk, :] * (SB_DH ** -0.5)
        qops.append([jnp.where((lane // SB_DH) == h, q, 0.0).astype(BF16) for h in range(2)])

    def run(kb, state, active):
        start = pl.multiple_of(kb * bk, bk)
        kblk = k_ref[pl.ds(start, bk), :].astype(BF16)
        vblk = v_ref[pl.ds(start, bk), :].astype(BF16)
        chains = [(s, h, m) for (s, m) in active for h in range(2)]

        def scores(c, _):
            s, h, _m = chains[c]
            return lax.dot_general(qops[s][h], kblk, (((1,), (1,)), ((), ())), preferred_element_type=F32)

        def logs(c, z):
            lb, l1m = _log_sig_pair(z)
            if chains[c][2]:
                l1m = jnp.where(causal, l1m, 0.0)
            hi = l1m.astype(BF16)
            return lb, l1m, hi, (l1m - hi.astype(F32)).astype(BF16)

        def suffix(c, v):
            lb, l1m, hi, lo = v
            rev = jnp.dot(hi, tri, preferred_element_type=F32) + jnp.dot(lo, tri, preferred_element_type=F32)
            return lb, l1m, rev

        def weights(c, v):
            s, h, m = chains[c]
            lb, l1m, rev = v
            tot = state[s * 2 + h][1]
            a = jnp.exp(lb + rev + tot)
            if m:
                a = jnp.where(causal, a, 0.0)
            return a.astype(BF16), tot + rev[:, 0:1] + l1m[:, 0:1]

        def values(c, v):
            s, h, _m = chains[c]
            a, tot = v
            return state[s * 2 + h][0] + jnp.dot(a, vblk, preferred_element_type=F32), tot

        outs = _wavefront(len(chains), [scores, logs, suffix, weights, values])
        new_state = list(state)
        for c, (s, h, _m) in enumerate(chains):
            new_state[s * 2 + h] = outs[c]
        return tuple(new_state)

    state = tuple((jnp.zeros((bk, LANES), F32), jnp.zeros((bk, 1), F32)) for _ in range(2 * nsub))
    for j, active in _causal_schedule(nsub):
        state = run(base + j, state, active)
    everyone = [(s, False) for s in range(nsub)]
    state = lax.fori_loop(0, base, lambda i, st: run(base - 1 - i, st, everyone), state)
    for s in range(nsub):
        o_ref[s * bk:(s + 1) * bk, :] = jnp.where(lane < SB_DH, state[2 * s][0], state[2 * s + 1][0])


def sb_prompt(sq, sk, sv, batch, seq, bk=256, nsub=2):
    bq = bk * nsub
    nq = seq // bq
    pairs = SB_HEADS // 2
    ri = lax.broadcasted_iota(I32, (bk, bk), 0)
    ci = lax.broadcasted_iota(I32, (bk, bk), 1)
    tri = (ri > ci).astype(BF16)
    return pl.pallas_call(
        functools.partial(_sb_prompt_kernel, bk=bk, nsub=nsub),
        grid=(batch, pairs, nq),
        in_specs=[pl.BlockSpec((bq, LANES), lambda b, p, i: (b * nq + i, p)),
                  pl.BlockSpec((seq, LANES), lambda b, p, i: (b, p)),
                  pl.BlockSpec((seq, LANES), lambda b, p, i: (b, p)),
                  pl.BlockSpec((bk, bk), lambda b, p, i: (0, 0))],
        out_specs=pl.BlockSpec((bq, LANES), lambda b, p, i: (b * nq + i, p)),
        out_shape=jax.ShapeDtypeStruct((batch * seq, SB_HEADS * SB_DH), F32),
        compiler_params=_params(("arbitrary", "arbitrary", "arbitrary")),
        name="sb_prompt",
    )(sq, sk, sv, tri)


def _lane_group_sums(x, ind_bf16):
    n, s, l = x.shape
    x2 = x.reshape(n * s, l)
    hi = x2.astype(BF16)
    lo = (x2 - hi.astype(F32)).astype(BF16)
    y = (jnp.dot(hi, ind_bf16, preferred_element_type=F32) +
         jnp.dot(lo, ind_bf16, preferred_element_type=F32))
    return y.reshape(n, s, ind_bf16.shape[1])


def _sb_decode_kernel(pt_ref, q_ref, *refs, pp):
    k_refs = refs[:pp]
    v_refs = refs[pp:2 * pp]
    tri_ref = refs[2 * pp]
    o_ref = refs[2 * pp + 1]
    acc_sc, tot_sc = refs[2 * pp + 2:]
    p = pl.program_id(1)

    @pl.when(p == 0)
    def _():
        acc_sc[...] = jnp.zeros_like(acc_sc)
        tot_sc[...] = jnp.zeros_like(tot_sc)

    tri = tri_ref[...]
    qcol = [q_ref[0, h] * (SB_DH ** -0.5) for h in range(SB_HEADS)]
    tot = tot_sc[...]
    for i in range(pp):
        z = jnp.concatenate([jnp.sum(k_refs[i][0, h] * qcol[h], axis=0, keepdims=True)
                             for h in range(SB_HEADS)], axis=0)
        lb, l1m = _log_sig_pair(z)
        rev = _suffix_sum(l1m, tri)
        a = jnp.exp(lb + rev + tot)
        tot = tot + rev[:, 0:1] + l1m[:, 0:1]
        for h in range(SB_HEADS):
            acc_sc[h] += v_refs[i][0, h] * a[h:h + 1, :]
    tot_sc[...] = tot

    @pl.when(p == pl.num_programs(1) - 1)
    def _():
        for h in range(SB_HEADS):
            o_ref[0, h] = jnp.sum(acc_sc[h], axis=1, keepdims=True)


def sb_decode(sq, cache_k, cache_v, page_table, pp=4):
    b = sq.shape[0]
    n_pages = page_table.shape[1]
    steps = n_pages // pp
    ck = jnp.transpose(cache_k, (0, 2, 3, 1))
    cv = jnp.transpose(cache_v, (0, 2, 3, 1))
    ri = lax.broadcasted_iota(I32, (PAGE, PAGE), 0)
    ci = lax.broadcasted_iota(I32, (PAGE, PAGE), 1)
    tri = (ri > ci).astype(BF16)

    def page_spec(i):
        return pl.BlockSpec((1, SB_HEADS, SB_DH, PAGE),
                            lambda bi, p, pt: (pt[bi * n_pages + (n_pages - 1 - (p * pp + i))], 0, 0, 0))

    col = pl.BlockSpec((1, SB_HEADS, SB_DH, 1), lambda bi, p, pt: (bi, 0, 0, 0))
    grid_spec = pltpu.PrefetchScalarGridSpec(
        num_scalar_prefetch=1,
        grid=(b, steps),
        in_specs=[col] + [page_spec(i) for i in range(pp)] + [page_spec(i) for i in range(pp)]
                 + [pl.BlockSpec((PAGE, PAGE), lambda bi, p, pt: (0, 0))],
        out_specs=col,
        scratch_shapes=[pltpu.VMEM((SB_HEADS, SB_DH, PAGE), F32), pltpu.VMEM((SB_HEADS, 1), F32)],
    )
    out = pl.pallas_call(
        functools.partial(_sb_decode_kernel, pp=pp),
        grid_spec=grid_spec,
        out_shape=jax.ShapeDtypeStruct((b, SB_HEADS, SB_DH, 1), F32),
        compiler_params=_params(("arbitrary", "arbitrary")),
        name="sb_decode",
    )(page_table.reshape(-1), sq.reshape(b, SB_HEADS, SB_DH, 1), *([ck] * pp), *([cv] * pp), tri)
    return out.reshape(b, SB_HEADS * SB_DH)


def _lambda_value(lam_ref):
    lp = lam_ref[...]
    s1 = jnp.sum(lp[0:1, :] * lp[1:2, :], axis=1, keepdims=True)
    s2 = jnp.sum(lp[2:3, :] * lp[3:4, :], axis=1, keepdims=True)
    return jnp.exp(s1) - jnp.exp(s2) + DF_LAMBDA_INIT


def _diff_prompt_kernel(q_ref, k_ref, v_ref, lam_ref, nw_ref, o_ref, *, bk, nsub):
    base = pl.program_id(2) * nsub
    lane = lax.broadcasted_iota(I32, (bk, LANES), 1)
    row = lax.broadcasted_iota(I32, (bk, bk), 0)
    colm = lax.broadcasted_iota(I32, (bk, bk), 1)
    causal = colm <= row
    qops = []
    for s in range(nsub):
        q = q_ref[s * bk:(s + 1) * bk, :] * (DF_D ** -0.5)
        qops.append([jnp.where((lane // DF_D) == c, q, 0.0).astype(BF16) for c in range(2)])

    def run(kb, state, active):
        start = pl.multiple_of(kb * bk, bk)
        kblk = k_ref[pl.ds(start, bk), :].astype(BF16)
        vblk = v_ref[pl.ds(start, bk), :].astype(BF16)
        chains = [(s, c, m) for (s, m) in active for c in range(2)]

        def scores(i, _):
            s, c, _m = chains[i]
            return lax.dot_general(qops[s][c], kblk, (((1,), (1,)), ((), ())), preferred_element_type=F32)

        def probs(i, z):
            s, c, m = chains[i]
            m_prev, l_prev, _ = state[s * 2 + c]
            if m:
                z = jnp.where(causal, z, NEG_INF)
            m_new = jnp.maximum(m_prev, jnp.max(z, axis=1, keepdims=True))
            p = jnp.exp(z - m_new)
            alpha = jnp.exp(m_prev - m_new)
            return m_new, alpha * l_prev + jnp.sum(p, axis=1, keepdims=True), alpha, p.astype(BF16)

        def values(i, v):
            m_new, l_new, alpha, p = v
            return m_new, l_new, alpha, jnp.dot(p, vblk, preferred_element_type=F32)

        def merge(i, v):
            s, c, _m = chains[i]
            m_new, l_new, alpha, pv = v
            return m_new, l_new, alpha * state[s * 2 + c][2] + pv

        outs = _wavefront(len(chains), [scores, probs, values, merge])
        new_state = list(state)
        for i, (s, c, _m) in enumerate(chains):
            new_state[s * 2 + c] = outs[i]
        return tuple(new_state)

    init = (jnp.full((bk, 1), NEG_INF, F32), jnp.zeros((bk, 1), F32), jnp.zeros((bk, LANES), F32))
    state = (init,) * (2 * nsub)
    for j, active in _causal_schedule(nsub):
        state = run(base + j, state, active)
    everyone = [(s, False) for s in range(nsub)]
    state = lax.fori_loop(0, base, lambda i, st: run(base - 1 - i, st, everyone), state)
    lam = _lambda_value(lam_ref)
    for s in range(nsub):
        (_, l0, a0), (_, l1, a1) = state[2 * s], state[2 * s + 1]
        o = a0 / l0 - lam * (a1 / l1)
        o_ref[s * bk:(s + 1) * bk, :] = _rms(o, nw_ref[...]) * (1.0 - DF_LAMBDA_INIT)


def diff_prompt(q, k, v, lam_pack, norm_w, batch, seq, bk=256, nsub=2):
    bq = bk * nsub
    nq = seq // bq
    return pl.pallas_call(
        functools.partial(_diff_prompt_kernel, bk=bk, nsub=nsub),
        grid=(batch, DF_HEADS, nq),
        in_specs=[pl.BlockSpec((bq, LANES), lambda b, h, i: (b * nq + i, h)),
                  pl.BlockSpec((seq, LANES), lambda b, h, i: (b, h)),
                  pl.BlockSpec((seq, LANES), lambda b, h, i: (b, h)),
                  pl.BlockSpec((8, DF_D), lambda b, h, i: (0, 0)),
                  pl.BlockSpec((1, DF_V), lambda b, h, i: (0, 0))],
        out_specs=pl.BlockSpec((bq, LANES), lambda b, h, i: (b * nq + i, h)),
        out_shape=jax.ShapeDtypeStruct((batch * seq, DF_HEADS * DF_V), F32),
        compiler_params=_params(("arbitrary", "arbitrary", "arbitrary")),
        name="diff_prompt",
    )(q, k, v, lam_pack, norm_w.reshape(1, DF_V))


def _diff_decode_kernel(pt_ref, q_ref, kn_ref, vn_ref, lam_ref, nw_ref, *refs, pp):
    k_refs = refs[:pp]
    v_refs = refs[pp:2 * pp]
    o_ref = refs[2 * pp]
    m_sc, l_sc, acc_sc = refs[2 * pp + 1:]
    p = pl.program_id(1)
    li = lax.broadcasted_iota(I32, (DF_V, 2 * DF_V), 0)
    ni = lax.broadcasted_iota(I32, (DF_V, 2 * DF_V), 1)
    ind = ((li // DF_D) == (ni // DF_V)).astype(BF16)
    q = q_ref[0] * (DF_D ** -0.5)

    @pl.when(p == 0)
    def _():
        z_own = _lane_group_sums((kn_ref[0] * q)[None], ind)[0]
        for c in range(2):
            m_sc[c] = z_own[:, c * DF_V:(c + 1) * DF_V]
            l_sc[c] = jnp.ones((DF_HEADS, DF_V), F32)
            acc_sc[c] = vn_ref[0]

    for i in range(pp):
        z = _lane_group_sums(k_refs[i][0] * q, ind)
        v = v_refs[i][0]
        for c in range(2):
            zc = z[:, :, c * DF_V:(c + 1) * DF_V]
            m_prev = m_sc[c]
            m_new = jnp.maximum(m_prev, jnp.max(zc, axis=0))
            pr = jnp.exp(zc - m_new)
            alpha = jnp.exp(m_prev - m_new)
            l_sc[c] = alpha * l_sc[c] + jnp.sum(pr, axis=0)
            acc_sc[c] = alpha * acc_sc[c] + jnp.sum(pr * v, axis=0)
            m_sc[c] = m_new

    @pl.when(p == pl.num_programs(1) - 1)
    def _():
        lam = _lambda_value(lam_ref)
        o = acc_sc[0] / l_sc[0] - lam * (acc_sc[1] / l_sc[1])
        o_ref[0] = _rms(o, nw_ref[...]) * (1.0 - DF_LAMBDA_INIT)


def diff_decode(q, k_new, v_new, cache_k, cache_v, page_table, lam_pack, norm_w, pp=4):
    b = q.shape[0]
    n_pages = page_table.shape[1]
    steps = n_pages // pp
    hd = (DF_HEADS, DF_V)

    def page_spec(i):
        return pl.BlockSpec((1, PAGE) + hd, lambda bi, p, pt: (pt[bi * n_pages + p * pp + i], 0, 0, 0))

    row = pl.BlockSpec((1,) + hd, lambda bi, p, pt: (bi, 0, 0))
    grid_spec = pltpu.PrefetchScalarGridSpec(
        num_scalar_prefetch=1,
        grid=(b, steps),
        in_specs=[row, row, row,
                  pl.BlockSpec((8, DF_D), lambda bi, p, pt: (0, 0)),
                  pl.BlockSpec((1, DF_V), lambda bi, p, pt: (0, 0))]
                 + [page_spec(i) for i in range(pp)] + [page_spec(i) for i in range(pp)],
        out_specs=row,
        scratch_shapes=[pltpu.VMEM((2,) + hd, F32), pltpu.VMEM((2,) + hd, F32), pltpu.VMEM((2,) + hd, F32)],
    )
    out = pl.pallas_call(
        functools.partial(_diff_decode_kernel, pp=pp),
        grid_spec=grid_spec,
        out_shape=jax.ShapeDtypeStruct((b,) + hd, F32),
        compiler_params=_params(("arbitrary", "arbitrary")),
        name="diff_decode",
    )(page_table.reshape(-1), q.reshape((b,) + hd), k_new.reshape((b,) + hd), v_new.reshape((b,) + hd),
      lam_pack, norm_w.reshape(1, DF_V), *([cache_k] * pp), *([cache_v] * pp))
    return out.reshape(b, DF_HEADS * DF_V)


IDX_NONE = 1e9
HI_HALF = -65536
SC_CORES, SC_SUBCORES = 2, 16
SC_CHUNK = 32


def _top_rows(tiles, ridx, k):
    n = tiles[0].shape[1]
    kio = lax.broadcasted_iota(I32, (k, n), 0)

    def body(i, carry):
        cur, vals, idxs = carry
        m = cur[0]
        for t in cur[1:]:
            m = jnp.maximum(m, t)
        m = jnp.max(m, axis=0, keepdims=True)
        f = None
        for t, ix in zip(cur, ridx):
            c = jnp.where(t == m, ix, IDX_NONE)
            f = c if f is None else jnp.minimum(f, c)
        f = jnp.min(f, axis=0, keepdims=True)
        cur = tuple(jnp.where(ix == f, NEG_INF, t) for t, ix in zip(cur, ridx))
        vals = jnp.where(kio == i, m, vals)
        idxs = jnp.where(kio == i, f, idxs)
        return cur, vals, idxs

    init = (tuple(tiles), jnp.zeros((k, n), F32), jnp.zeros((k, n), F32))
    _, vals, idxs = lax.fori_loop(0, k, body, init)
    return vals, idxs


def _pair_candidates(s1, s2):
    k, n = s1.shape
    sub = 8
    r8 = lax.broadcasted_iota(I32, (sub, n), 0)
    r8f = r8.astype(F32)
    tiles, ridx = [], []
    for a in range(k):
        nb = k // (a + 1)
        if nb == 1:
            break
        for b0 in range(0, nb, sub):
            t = s1[a:a + 1, :] + s2[b0:b0 + sub, :]
            ix = r8f + float(a * k + b0)
            if b0 + sub > nb:
                keep = r8 < (nb - b0)
                t = jnp.where(keep, t, NEG_INF)
                ix = jnp.where(keep, ix, IDX_NONE)
            tiles.append(t)
            ridx.append(ix)
    a0 = a
    assert (k - a0) % sub == 0
    for a1 in range(a0, k, sub):
        tiles.append(s1[a1:a1 + sub, :] + s2[0:1, :])
        ridx.append((r8f + float(a1)) * float(k))
    return tiles, ridx


def _pick_rows(table, sel):
    kk = table.shape[0]
    out = jnp.zeros(sel.shape, F32)
    for a in range(kk):
        out = jnp.where(sel == float(a), table[a:a + 1, :], out)
    return out


def _peer_select_kernel(h_ref, nw_ref, wq_ref, keys_ref, xn_ref, e_ref, g_ref, e_sc):
    kk = PEER_TOPK
    xn = _rms(h_ref[...], nw_ref[...])
    xn_ref[...] = xn
    q = jnp.dot(xn.astype(BF16), wq_ref[...], preferred_element_type=F32)
    tb = q.shape[0]
    key_idx = [lax.broadcasted_iota(I32, (PEER_NKEYS, tb), 0).astype(F32)]
    for h in range(PEER_HEADS):
        tops = []
        for c in range(2):
            j = h * 2 + c
            s_t = _bdot_nt(keys_ref[c], q[:, j * LANES:(j + 1) * LANES])
            tops.append(_top_rows([s_t], key_idx, kk))
        (s1, i1), (s2, i2) = tops
        c_top, c_idx = _top_rows(*_pair_candidates(s1, s2), kk)
        ia = jnp.floor(c_idx * (1.0 / kk))
        ib = c_idx - ia * kk
        e = _pick_rows(i1, ia) * float(PEER_NKEYS) + _pick_rows(i2, ib)
        ex = jnp.exp(c_top - jnp.max(c_top, axis=0, keepdims=True))
        g = ex / jnp.sum(ex, axis=0, keepdims=True)
        e_sc[h * kk:(h + 1) * kk, :] = e
        for blk in range(tb // LANES):
            g_ref[blk, h * kk:(h + 1) * kk, :] = g[:, blk * LANES:(blk + 1) * LANES]
    e_ref[...] = e_sc[...].T.astype(I32)


def peer_select(h, nw, wq_bf16, keys):
    t, d = h.shape
    tb = 2 * LANES if t % (2 * LANES) == 0 else LANES
    picks = PEER_HEADS * PEER_TOPK
    gb = tb // LANES
    return pl.pallas_call(
        _peer_select_kernel,
        grid=(t // tb,),
        in_specs=[pl.BlockSpec((tb, d), lambda i: (i, 0)),
                  pl.BlockSpec((1, d), lambda i: (0, 0)),
                  pl.BlockSpec(wq_bf16.shape, lambda i: (0, 0)),
                  pl.BlockSpec(keys.shape, lambda i: (0, 0, 0))],
        out_specs=[pl.BlockSpec((tb, d), lambda i: (i, 0)),
                   pl.BlockSpec((tb, picks), lambda i: (i, 0)),
                   pl.BlockSpec((gb, picks, LANES), lambda i: (i, 0, 0))],
        out_shape=[jax.ShapeDtypeStruct((t, d), F32),
                   jax.ShapeDtypeStruct((t, picks), I32),
                   jax.ShapeDtypeStruct((t // LANES, picks, LANES), F32)],
        scratch_shapes=[pltpu.VMEM((picks, tb), F32)],
        compiler_params=_params(("arbitrary",)),
        name="peer_select",
    )(h, nw.reshape(1, d), wq_bf16, keys)


def _peer_gather_kernel(e_hbm, uv_hbm, xn_ref, g_ref, h_ref, o_ref, ids_smem, rows, ids_sem, row_sem,
                        *, n_tok, n_slots):
    i = pl.program_id(0)
    nblk = pl.num_programs(0)
    picks = PEER_HEADS * PEER_TOPK
    d = xn_ref.shape[1]
    tb = xn_ref.shape[0]
    ahead = n_slots - 1
    assert n_tok % n_slots == 0 and n_tok > ahead
    cur = i % 2
    has_next = i + 1 < nblk

    def ids_copy(blk, buf):
        return pltpu.make_async_copy(e_hbm.at[pl.ds(blk * tb, tb)], ids_smem.at[buf], ids_sem.at[buf])

    def issue(buf, t, slot, j0, j1):
        for j in range(j0, j1):
            e = ids_smem[buf, t, j]
            pltpu.make_async_copy(uv_hbm.at[e], rows.at[slot, pl.ds(j, 1)],
                                  row_sem.at[slot]).start(priority=j % 2)

    def wait(slot):
        pltpu.make_async_copy(rows.at[slot], rows.at[slot], row_sem.at[slot]).wait()

    lane = lax.broadcasted_iota(I32, (picks, LANES), 1)
    half = picks // 2

    def token(t, slot, issue_half):
        wait(slot)
        x = xn_ref[pl.ds(t, 1), :]
        u = lax.bitcast_convert_type(rows[slot] & HI_HALF, F32)
        prod = u * x
        part = prod[:, 0:LANES]
        for c in range(1, d // LANES):
            part = part + prod[:, c * LANES:(c + 1) * LANES]
        issue_half(0, half)
        act = jnp.sum(part, axis=1, keepdims=True)
        g = jnp.sum(jnp.where(lane == t, g_ref[0], 0.0), axis=1, keepdims=True)
        wgt = g * jax.nn.gelu(act)
        v = lax.bitcast_convert_type(rows[slot] << 16, F32)
        mix = jnp.sum(wgt * v, axis=0, keepdims=True)
        issue_half(half, picks)
        o_ref[pl.ds(t, 1), :] = h_ref[pl.ds(t, 1), :] + mix

    @pl.when(i == 0)
    def _():
        first = ids_copy(0, 0)
        first.start()
        first.wait()
        for t in range(ahead):
            issue(0, t, t, 0, picks)

    @pl.when(has_next)
    def _():
        ids_copy(i + 1, 1 - cur).start()

    if n_tok < tb:
        o_ref[...] = h_ref[...]

    n_groups = n_tok // n_slots

    def group(gi, c):
        for u in range(n_slots):
            t = gi * n_slots + u
            token(t, u, functools.partial(issue, cur, t + ahead, (u + ahead) % n_slots))
        return c

    lax.fori_loop(0, n_groups - 1, group, 0)

    @pl.when(has_next)
    def _():
        ids_copy(i + 1, 1 - cur).wait()

    for u in range(n_slots):
        t = (n_groups - 1) * n_slots + u
        if u == 0:
            token(t, u, functools.partial(issue, cur, t + ahead, (u + ahead) % n_slots))
        else:
            def issue_next(j0, j1, u=u):
                @pl.when(has_next)
                def _():
                    issue(1 - cur, u - 1, (u + ahead) % n_slots, j0, j1)

            token(t, u, issue_next)


def pack_uv(u, v):
    ub = lax.bitcast_convert_type(u.astype(BF16), jnp.uint16).astype(jnp.uint32)
    vb = lax.bitcast_convert_type(v.astype(BF16), jnp.uint16).astype(jnp.uint32)
    return lax.bitcast_convert_type((ub << 16) | vb, I32)


def peer_gather(e_ids, g_blocks, uv, xn, h, n_tok, nb):
    t, d = h.shape
    tb = LANES
    assert nb == 1 or n_tok == tb
    picks = PEER_HEADS * PEER_TOPK
    n_slots = 8
    return pl.pallas_call(
        functools.partial(_peer_gather_kernel, n_tok=n_tok, n_slots=n_slots),
        grid=(nb,),
        in_specs=[pl.BlockSpec(memory_space=pl.ANY),
                  pl.BlockSpec(memory_space=pl.ANY),
                  pl.BlockSpec((tb, d), lambda i: (i, 0)),
                  pl.BlockSpec((1, picks, tb), lambda i: (i, 0, 0)),
                  pl.BlockSpec((tb, d), lambda i: (i, 0))],
        out_specs=pl.BlockSpec((tb, d), lambda i: (i, 0)),
        out_shape=jax.ShapeDtypeStruct((nb * tb, d), F32),
        scratch_shapes=[pltpu.SMEM((2, tb, picks), I32),
                        pltpu.VMEM((n_slots, picks, d), I32),
                        pltpu.SemaphoreType.DMA((2,)),
                        pltpu.SemaphoreType.DMA((n_slots,))],
        compiler_params=_params(("arbitrary",)),
        name="peer_gather",
    )(e_ids, uv, xn, g_blocks, h)


def _sc_gather_kernel(table_hbm, idx_hbm, out_hbm, idx0, idx1, rows0, rows1, sem0, sem1, *, per_worker):
    wid = lax.axis_index("s") * SC_CORES + lax.axis_index("c")
    base = wid * per_worker
    n_chunks = per_worker // SC_CHUNK
    bufs = ((idx0, rows0, sem0), (idx1, rows1, sem1))

    def gather(b):
        idx_v, rows_v, sem = bufs[b]
        return pltpu.make_async_copy(table_hbm.at[idx_v], rows_v, sem)

    def request(c, b):
        pltpu.sync_copy(idx_hbm.at[pl.ds(base + c * SC_CHUNK, SC_CHUNK)], bufs[b][0])
        gather(b).start()

    request(0, 0)

    def pair(k, carry):
        for b in range(2):
            c = 2 * k + b

            @pl.when(c + 1 < n_chunks)
            def _():
                request(c + 1, 1 - b)

            gather(b).wait()
            pltpu.sync_copy(bufs[b][1], out_hbm.at[pl.ds(base + c * SC_CHUNK, SC_CHUNK)])
        return carry

    lax.fori_loop(0, n_chunks // 2, pair, 0)


def sc_gather_rows(table, idx):
    n, = idx.shape
    d = table.shape[1]
    workers = SC_CORES * SC_SUBCORES
    per_worker = n // workers
    assert per_worker * workers == n and per_worker % (2 * SC_CHUNK) == 0
    mesh = plsc.VectorSubcoreMesh(core_axis_name="c", subcore_axis_name="s",
                                  num_cores=SC_CORES, num_subcores=SC_SUBCORES)
    run = pl.kernel(
        functools.partial(_sc_gather_kernel, per_worker=per_worker),
        out_type=jax.ShapeDtypeStruct((n, d), table.dtype),
        mesh=mesh,
        scratch_types=[pltpu.VMEM((SC_CHUNK,), I32), pltpu.VMEM((SC_CHUNK,), I32),
                       pltpu.VMEM((SC_CHUNK, d), table.dtype), pltpu.VMEM((SC_CHUNK, d), table.dtype),
                       pltpu.SemaphoreType.DMA, pltpu.SemaphoreType.DMA],
        name="sc_gather_rows",
    )
    return run(table, idx)


def _peer_mix_kernel(rows_ref, xn_ref, g_ref, h_ref, o_ref, *, tok0, tg):
    i = pl.program_id(0)
    picks = PEER_HEADS * PEER_TOPK
    d = xn_ref.shape[1]
    lane = lax.broadcasted_iota(I32, (picks, LANES), 1)
    g_tile = g_ref[0]
    for t in range(tg):
        words = rows_ref[t * picks:(t + 1) * picks, :]
        x = xn_ref[t:t + 1, :]
        prod = lax.bitcast_convert_type(words & HI_HALF, F32) * x
        part = prod[:, 0:LANES]
        for c in range(1, d // LANES):
            part = part + prod[:, c * LANES:(c + 1) * LANES]
        act = jnp.sum(part, axis=1, keepdims=True)
        tok_lane = (tok0 + i * tg + t) % LANES
        g = jnp.sum(jnp.where(lane == tok_lane, g_tile, 0.0), axis=1, keepdims=True)
        wgt = g * jax.nn.gelu(act)
        mix = jnp.sum(wgt * lax.bitcast_convert_type(words << 16, F32), axis=0, keepdims=True)
        o_ref[t:t + 1, :] = h_ref[t:t + 1, :] + mix


def peer_mix_rows(rows, g_blocks, xn, h, tok0):
    t, d = h.shape
    picks = PEER_HEADS * PEER_TOPK
    n = rows.shape[0] // picks
    tg = 8
    assert n % tg == 0 and tok0 % tg == 0 and LANES % tg == 0
    off = tok0 // tg
    return pl.pallas_call(
        functools.partial(_peer_mix_kernel, tok0=tok0, tg=tg),
        grid=(n // tg,),
        in_specs=[pl.BlockSpec((tg * picks, d), lambda i: (i, 0)),
                  pl.BlockSpec((tg, d), lambda i: (off + i, 0)),
                  pl.BlockSpec((1, picks, LANES), lambda i: ((tok0 + i * tg) // LANES, 0, 0)),
                  pl.BlockSpec((tg, d), lambda i: (off + i, 0))],
        out_specs=pl.BlockSpec((tg, d), lambda i: (i, 0)),
        out_shape=jax.ShapeDtypeStruct((n, d), F32),
        compiler_params=_params(("arbitrary",)),
        name="peer_mix_rows",
    )(rows, xn, g_blocks, h)


def _sc_shares(t):
    return ((t // 16) * 3,) * 3 if t % 8192 == 0 else ()


def peer_layer(h, nw, wq_bf16, keys, uv):
    t, d = h.shape
    tp = -(-t // LANES) * LANES
    hp = h if tp == t else jnp.pad(h, ((0, tp - t), (0, 0)))
    xn, e_ids, g_blocks = peer_select(hp, nw, wq_bf16, keys)
    shares = _sc_shares(tp)
    t_tc = tp - sum(shares)
    gathered, tok0 = [], t_tc
    for n in shares:
        gathered.append((tok0, sc_gather_rows(uv, e_ids[tok0:tok0 + n].reshape(-1))))
        tok0 += n
    outs = [peer_gather(e_ids, g_blocks, uv.reshape(-1, 1, d), xn, hp, min(t, LANES), t_tc // LANES)]
    outs += [peer_mix_rows(rows, g_blocks, xn, hp, start) for start, rows in gathered]
    out = outs[0] if len(outs) == 1 else jnp.concatenate(outs, axis=0)
    return out if tp == t else out[:t]


def kernel(x_prompt, x_sample, p_prompt, p_sample, state_mlstm_C, state_mlstm_n, state_mlstm_m, cache_sb_k, cache_sb_v, cache_diff_k, cache_diff_v, page_table, norm_mix0, w_in0, b_igate0, b_fgate0, mlstm_norm0, w_out0, norm_ffn0, peer_wq0, peer_keys0, peer_u0, peer_v0, ple_gate0, ple_proj0, norm_mix1, w_qkv1, lambda_q1, lambda_k1, lambda_q2, lambda_k2, diff_norm1, w_out1, norm_ffn1, peer_wq1, peer_keys1, peer_u1, peer_v1, ple_gate1, ple_proj1, final_norm):
    bp, sp, d = x_prompt.shape
    bs, ss, _ = x_sample.shape
    assert ss == 1
    ml_w = ML_HEADS * ML_DH
    sb_w = SB_HEADS * SB_DH
    df_w = DF_HEADS * DF_V

    g0 = 4 * ml_w
    g1 = g0 + 2 * ML_HEADS
    gates_w = jnp.pad(w_in0[:, g0:g1], ((0, 0), (0, LANES - 2 * ML_HEADS)))
    w_in = jnp.concatenate([w_in0[:, :g0], w_in0[:, g1:], gates_w], axis=1).astype(BF16)
    in_splits = (4 * ml_w, sb_w, sb_w, sb_w, LANES)
    gate_bias = jnp.pad(jnp.concatenate([b_igate0, b_fgate0]), (0, LANES - 2 * ML_HEADS)).reshape(1, LANES)
    w_out0_b = w_out0.astype(BF16)
    w_qkv = w_qkv1.astype(BF16)
    w_out1_b = w_out1.astype(BF16)
    lam_pack = jnp.pad(jnp.stack([lambda_q1, lambda_k1, lambda_q2, lambda_k2]), ((0, 4), (0, 0)))
    uv0 = pack_uv(peer_u0, peer_v0)
    uv1 = pack_uv(peer_u1, peer_v1)
    peer0 = (norm_ffn0, peer_wq0.astype(BF16), peer_keys0, uv0)
    peer1 = (norm_ffn1, peer_wq1.astype(BF16), peer_keys1, uv1)
    ple0 = (ple_gate0.astype(BF16), ple_proj0.astype(BF16))
    ple1 = (ple_gate1.astype(BF16), ple_proj1.astype(BF16))

    def layer0_tail(h, mix_acts, p0):
        h = proj_residual(h, mix_acts, [w_out0_b[:ml_w], w_out0_b[ml_w:]])
        h = peer_layer(h, *peer0)
        return ple(h, p0, *ple0, final_norm, False)

    def layer1_tail(h, o, p1):
        h = proj_residual(h, [o], [w_out1_b])
        h = peer_layer(h, *peer1)
        return ple(h, p1, *ple1, final_norm, True)

    tp = bp * sp
    h = x_prompt.reshape(tp, d)
    ml, sq, sk, sv, gates = norm_proj(h, norm_mix0, w_in, in_splits)
    h_ml, c_p, n_p, m_p = mlstm_prompt(ml, gates, gate_bias, mlstm_norm0, bp, sp)
    h_sb = sb_prompt(sq, sk, sv, bp, sp)
    h = layer0_tail(h, [h_ml, h_sb], p_prompt[0].reshape(tp, -1))
    q1, k1, v1 = norm_proj(h, norm_mix1, w_qkv, (df_w, df_w, df_w))
    o = diff_prompt(q1, k1, v1, lam_pack, diff_norm1, bp, sp)
    y_prompt = layer1_tail(h, o, p_prompt[1].reshape(tp, -1)).reshape(bp, sp, d)

    hs = x_sample.reshape(bs, d)
    ml_s, sq_s, sk_s, sv_s, gates_s = norm_proj(hs, norm_mix0, w_in, in_splits)
    h_ml_s, c_s, n_s, m_s = mlstm_step(ml_s, gates_s, gate_bias, mlstm_norm0,
                                       state_mlstm_C, state_mlstm_n, state_mlstm_m)
    h_sb_s = sb_decode(sq_s, cache_sb_k, cache_sb_v, page_table)
    hs = layer0_tail(hs, [h_ml_s, h_sb_s], p_sample[0].reshape(bs, -1))
    q1s, k1s, v1s = norm_proj(hs, norm_mix1, w_qkv, (df_w, df_w, df_w))
    o_s = diff_decode(q1s, k1s, v1s, cache_diff_k, cache_diff_v, page_table, lam_pack, diff_norm1)
    y_sample = layer1_tail(hs, o_s, p_sample[1].reshape(bs, -1)).reshape(bs, ss, d)

    return (y_prompt, y_sample,
            c_p, n_p, m_p.reshape(bp, ML_HEADS),
            sk.reshape(bp, sp, SB_HEADS, SB_DH), sv.reshape(bp, sp, SB_HEADS, SB_DH),
            k1.reshape(bp, sp, DF_HEADS, DF_V), v1.reshape(bp, sp, DF_HEADS, DF_V),
            c_s, n_s, m_s,
            sk_s.reshape(bs, ss, SB_HEADS, SB_DH), sv_s.reshape(bs, ss, SB_HEADS, SB_DH),
            k1s.reshape(bs, ss, DF_HEADS, DF_V), v1s.reshape(bs, ss, DF_HEADS, DF_V))
```

```python
import functools
import math

import jax
import jax.numpy as jnp
from jax import lax
from jax.experimental import pallas as pl
from jax.experimental.pallas import tpu as pltpu

F32 = jnp.float32
BF16 = jnp.bfloat16
I32 = jnp.int32

RMS_EPS = 1e-6
LANES = 128
ML_HEADS, ML_DH, ML_CHUNK = 4, 128, 64
SB_HEADS, SB_DH = 8, 64
DF_HEADS, DF_D, DF_V = 8, 64, 128
DF_LAMBDA_INIT = 0.8 - 0.6 * math.exp(-0.3 * 1)
PEER_HEADS, PEER_NKEYS, PEER_TOPK = 8, 128, 16
PAGE = 128
VMEM_LIMIT = 56 * 1024 * 1024
NEG_INF = float("-inf")


def _params(sem):
    return pltpu.CompilerParams(dimension_semantics=sem, vmem_limit_bytes=VMEM_LIMIT)


def _bdot(a, b):
    return jnp.dot(a.astype(BF16), b.astype(BF16), preferred_element_type=F32)


def _bdot_nt(a, b):
    return lax.dot_general(a.astype(BF16), b.astype(BF16), (((1,), (1,)), ((), ())),
                           preferred_element_type=F32)


def _rms(x, w):
    return x * lax.rsqrt(jnp.mean(x * x, axis=-1, keepdims=True) + RMS_EPS) * w


def _row_tile(t):
    for tm in (256, 128, 64, 32, 16, 8):
        if t % tm == 0:
            return tm
    raise ValueError(f"row count {t} must be a multiple of 8")


def _norm_proj_kernel(x_ref, nw_ref, w_ref, *o_refs, splits):
    xn = _rms(x_ref[...], nw_ref[...]).astype(BF16)
    off = 0
    for o_ref, n in zip(o_refs, splits):
        o_ref[...] = jnp.dot(xn, w_ref[:, off:off + n], preferred_element_type=F32)
        off += n


def norm_proj(x, nw, w_bf16, splits):
    t, d = x.shape
    n = w_bf16.shape[1]
    assert sum(splits) == n and all(s % LANES == 0 for s in splits)
    tm = _row_tile(t)
    return pl.pallas_call(
        functools.partial(_norm_proj_kernel, splits=tuple(splits)),
        grid=(t // tm,),
        in_specs=[pl.BlockSpec((tm, d), lambda i: (i, 0)),
                  pl.BlockSpec((1, d), lambda i: (0, 0)),
                  pl.BlockSpec((d, n), lambda i: (0, 0))],
        out_specs=[pl.BlockSpec((tm, s), lambda i: (i, 0)) for s in splits],
        out_shape=[jax.ShapeDtypeStruct((t, s), F32) for s in splits],
        compiler_params=_params(("arbitrary",)),
        name="norm_proj",
    )(x, nw.reshape(1, d), w_bf16)


def _proj_res_kernel(*refs, n_in):
    res_ref = refs[0]
    a_refs = refs[1:1 + n_in]
    w_refs = refs[1 + n_in:1 + 2 * n_in]
    o_ref = refs[1 + 2 * n_in]
    acc = res_ref[...]
    for a_ref, w_ref in zip(a_refs, w_refs):
        acc = acc + jnp.dot(a_ref[...].astype(BF16), w_ref[...], preferred_element_type=F32)
    o_ref[...] = acc


def proj_residual(res, acts, ws_bf16):
    t, d = res.shape
    tm = _row_tile(t)
    n_in = len(acts)
    in_specs = [pl.BlockSpec((tm, d), lambda i: (i, 0))]
    in_specs += [pl.BlockSpec((tm, a.shape[1]), lambda i: (i, 0)) for a in acts]
    in_specs += [pl.BlockSpec(w.shape, lambda i: (0, 0)) for w in ws_bf16]
    return pl.pallas_call(
        functools.partial(_proj_res_kernel, n_in=n_in),
        grid=(t // tm,),
        in_specs=in_specs,
        out_specs=pl.BlockSpec((tm, d), lambda i: (i, 0)),
        out_shape=jax.ShapeDtypeStruct((t, d), F32),
        compiler_params=_params(("arbitrary",)),
        name="proj_residual",
    )(res, *acts, *ws_bf16)


def _ple_kernel(h_ref, p_ref, wg_ref, wp_ref, fn_ref, o_ref, *, final_norm):
    h = h_ref[...]
    gate = jax.nn.sigmoid(jnp.dot(h.astype(BF16), wg_ref[...], preferred_element_type=F32))
    emb = jnp.dot(p_ref[...].astype(BF16), wp_ref[...], preferred_element_type=F32)
    y = h + gate * emb
    if final_norm:
        y = _rms(y, fn_ref[...])
    o_ref[...] = y


def ple(h, p, wg_bf16, wp_bf16, fnorm, final_norm):
    t, d = h.shape
    pd = p.shape[1]
    tm = _row_tile(t)
    return pl.pallas_call(
        functools.partial(_ple_kernel, final_norm=final_norm),
        grid=(t // tm,),
        in_specs=[pl.BlockSpec((tm, d), lambda i: (i, 0)),
                  pl.BlockSpec((tm, pd), lambda i: (i, 0)),
                  pl.BlockSpec((d, d), lambda i: (0, 0)),
                  pl.BlockSpec((pd, d), lambda i: (0, 0)),
                  pl.BlockSpec((1, d), lambda i: (0, 0))],
        out_specs=pl.BlockSpec((tm, d), lambda i: (i, 0)),
        out_shape=jax.ShapeDtypeStruct((t, d), F32),
        compiler_params=_params(("arbitrary",)),
        name="ple",
    )(h, p, wg_bf16, wp_bf16, fnorm.reshape(1, d))


def _mlstm_chunk_kernel(q_ref, k_ref, v_ref, o_ref, g_ref, gb_ref, nw_ref,
                        h_ref, c_out, n_out, m_out, c_sc, n_sc, m_sc):
    ci = pl.program_id(1)
    L = ML_CHUNK

    @pl.when(ci == 0)
    def _():
        c_sc[...] = jnp.zeros_like(c_sc)
        n_sc[...] = jnp.zeros_like(n_sc)
        m_sc[...] = jnp.zeros_like(m_sc)

    hp = lax.Precision.HIGHEST
    g = g_ref[...] + gb_ref[...]
    lf = jax.nn.log_sigmoid(g)
    ri = lax.broadcasted_iota(I32, (L, L), 0)
    cj = lax.broadcasted_iota(I32, (L, L), 1)
    tril = (cj <= ri)
    low = tril.astype(F32)
    upp = (ri <= cj).astype(F32)
    eye = (lax.broadcasted_iota(I32, (LANES, LANES), 0) ==
           lax.broadcasted_iota(I32, (LANES, LANES), 1)).astype(F32)
    nt = (((1,), (1,)), ((), ()))
    g_t = lax.dot_general(eye, g, nt, precision=hp, preferred_element_type=F32)
    lf_t = lax.dot_general(eye, lf, nt, precision=hp, preferred_element_type=F32)
    b_c = jnp.dot(low, lf, precision=hp, preferred_element_type=F32)
    b_r = jnp.dot(lf_t, upp, precision=hp, preferred_element_type=F32)

    for h in range(ML_HEADS):
        sl = slice(h * ML_DH, (h + 1) * ML_DH)
        q = q_ref[:, sl] * (ML_DH ** -0.5)
        k = k_ref[:, sl]
        v = v_ref[:, sl]
        li_c = g[:, h:h + 1]
        li_r = g_t[h:h + 1, :]
        bc = b_c[:, ML_HEADS + h:ML_HEADS + h + 1]
        br = b_r[ML_HEADS + h:ML_HEADS + h + 1, :]
        m_prev = m_sc[:, h:h + 1]
        c_prev = c_sc[h]
        n_prev = n_sc[h:h + 1, :]
        dmat = jnp.where(tril, bc - br + li_r, NEG_INF)
        inter = bc + m_prev
        m_t = jnp.maximum(inter, jnp.max(dmat, axis=1, keepdims=True))
        w_inter = jnp.exp(inter - m_t)
        wqk = jnp.exp(dmat - m_t) * _bdot_nt(q, k)
        num = w_inter * _bdot_nt(q, c_prev) + _bdot(wqk, v)
        den = w_inter * jnp.sum(q * n_prev, axis=1, keepdims=True) + jnp.sum(wqk, axis=1, keepdims=True)
        hh = num / jnp.maximum(jnp.abs(den), jnp.exp(-m_t))
        b_last = br[:, L - 1:L]
        g_r = b_last - br + li_r
        g_c = b_last - bc + li_c
        m_new = jnp.maximum(b_last + m_prev, jnp.max(g_r, axis=1, keepdims=True))
        a = jnp.exp(b_last + m_prev - m_new)
        wg_c = jnp.exp(g_c - m_new)
        wv = (wg_c * v).astype(BF16)
        c_new = a * c_prev + lax.dot_general(wv, k.astype(BF16), (((0,), (0,)), ((), ())),
                                             preferred_element_type=F32)
        n_new = a * n_prev + jnp.sum(wg_c * k, axis=0, keepdims=True)
        c_sc[h] = c_new
        n_sc[h:h + 1, :] = n_new
        m_sc[:, h:h + 1] = m_new
        hn = _rms(hh, nw_ref[:, sl])
        h_ref[:, sl] = jax.nn.sigmoid(o_ref[:, sl]) * hn

    @pl.when(ci == pl.num_programs(1) - 1)
    def _():
        c_out[0] = c_sc[...]
        n_out[0] = n_sc[...]
        m_out[0] = m_sc[:, 0:ML_HEADS]


def mlstm_prompt(ml, gates, gate_bias, norm_w, batch, seq):
    L = ML_CHUNK
    nc = seq // L
    w = ML_HEADS * ML_DH
    col = lambda j: pl.BlockSpec((L, w), lambda b, c: (b * nc + c, j))
    return pl.pallas_call(
        _mlstm_chunk_kernel,
        grid=(batch, nc),
        in_specs=[col(0), col(1), col(2), col(3),
                  pl.BlockSpec((L, LANES), lambda b, c: (b * nc + c, 0)),
                  pl.BlockSpec((1, LANES), lambda b, c: (0, 0)),
                  pl.BlockSpec((1, w), lambda b, c: (0, 0))],
        out_specs=[pl.BlockSpec((L, w), lambda b, c: (b * nc + c, 0)),
                   pl.BlockSpec((1, ML_HEADS, ML_DH, ML_DH), lambda b, c: (b, 0, 0, 0)),
                   pl.BlockSpec((1, ML_HEADS, ML_DH), lambda b, c: (b, 0, 0)),
                   pl.BlockSpec((1, 1, ML_HEADS), lambda b, c: (b, 0, 0))],
        out_shape=[jax.ShapeDtypeStruct((batch * seq, w), F32),
                   jax.ShapeDtypeStruct((batch, ML_HEADS, ML_DH, ML_DH), F32),
                   jax.ShapeDtypeStruct((batch, ML_HEADS, ML_DH), F32),
                   jax.ShapeDtypeStruct((batch, 1, ML_HEADS), F32)],
        scratch_shapes=[pltpu.VMEM((ML_HEADS, ML_DH, ML_DH), F32),
                        pltpu.VMEM((ML_HEADS, ML_DH), F32),
                        pltpu.VMEM((1, LANES), F32)],
        compiler_params=_params(("arbitrary", "arbitrary")),
        name="mlstm_chunk",
    )(ml, ml, ml, ml, gates, gate_bias, norm_w.reshape(1, w))


def _mlstm_step_kernel(x_ref, g_ref, gb_ref, nw_ref, c_ref, n_ref, m_ref,
                       h_ref, c_out, n_out, m_out):
    w = ML_HEADS * ML_DH
    g = g_ref[0] + gb_ref[...]
    lf = jax.nn.log_sigmoid(g)
    eye = (lax.broadcasted_iota(I32, (ML_DH, ML_DH), 0) ==
           lax.broadcasted_iota(I32, (ML_DH, ML_DH), 1)).astype(F32)

    def to_col(row):
        return jnp.sum(eye * row, axis=1, keepdims=True)

    def to_row(colv):
        return jnp.sum(eye * colv, axis=0, keepdims=True)

    x = x_ref[0]
    for h in range(ML_HEADS):
        q = x[:, h * ML_DH:(h + 1) * ML_DH] * (ML_DH ** -0.5)
        k = x[:, w + h * ML_DH:w + (h + 1) * ML_DH]
        v = x[:, 2 * w + h * ML_DH:2 * w + (h + 1) * ML_DH]
        o = x[:, 3 * w + h * ML_DH:3 * w + (h + 1) * ML_DH]
        li = g[:, h:h + 1]
        lfh = lf[:, ML_HEADS + h:ML_HEADS + h + 1]
        m_prev = m_ref[0][:, h:h + 1]
        c_prev = c_ref[0, h]
        n_prev = n_ref[0][h:h + 1, :]
        inter = lfh + m_prev
        m_t = jnp.maximum(inter, li)
        w_inter = jnp.exp(inter - m_t)
        wqk = jnp.exp(li - m_t) * jnp.sum(q * k, axis=1, keepdims=True)
        v_col = to_col(v)
        num = w_inter * jnp.sum(c_prev * q, axis=1, keepdims=True) + wqk * v_col
        den = w_inter * jnp.sum(n_prev * q, axis=1, keepdims=True) + wqk
        hh = num / jnp.maximum(jnp.abs(den), jnp.exp(-m_t))
        m_new = m_t
        a = jnp.exp(inter - m_new)
        wg = jnp.exp(li - m_new)
        c_out[0, h] = a * c_prev + (wg * v_col) * k
        n_out[0, h:h + 1, :] = a * n_prev + wg * k
        m_out[0, :, h:h + 1] = m_new
        hr = to_row(hh)
        hn = _rms(hr, nw_ref[:, h * ML_DH:(h + 1) * ML_DH])
        h_ref[0, :, h * ML_DH:(h + 1) * ML_DH] = jax.nn.sigmoid(o) * hn


def mlstm_step(ml, gates, gate_bias, norm_w, c0, n0, m0):
    b = ml.shape[0]
    w = ML_HEADS * ML_DH
    row = lambda n: pl.BlockSpec((1, 1, n), lambda i: (i, 0, 0))
    h, c, n, m = pl.pallas_call(
        _mlstm_step_kernel,
        grid=(b,),
        in_specs=[row(4 * w), row(LANES),
                  pl.BlockSpec((1, LANES), lambda i: (0, 0)),
                  pl.BlockSpec((1, w), lambda i: (0, 0)),
                  pl.BlockSpec((1, ML_HEADS, ML_DH, ML_DH), lambda i: (i, 0, 0, 0)),
                  pl.BlockSpec((1, ML_HEADS, ML_DH), lambda i: (i, 0, 0)),
                  row(ML_HEADS)],
        out_specs=[row(w),
                   pl.BlockSpec((1, ML_HEADS, ML_DH, ML_DH), lambda i: (i, 0, 0, 0)),
                   pl.BlockSpec((1, ML_HEADS, ML_DH), lambda i: (i, 0, 0)),
                   row(ML_HEADS)],
        out_shape=[jax.ShapeDtypeStruct((b, 1, w), F32),
                   jax.ShapeDtypeStruct((b, ML_HEADS, ML_DH, ML_DH), F32),
                   jax.ShapeDtypeStruct((b, ML_HEADS, ML_DH), F32),
                   jax.ShapeDtypeStruct((b, 1, ML_HEADS), F32)],
        compiler_params=_params(("arbitrary",)),
        name="mlstm_step",
    )(ml.reshape(b, 1, 4 * w), gates.reshape(b, 1, LANES), gate_bias, norm_w.reshape(1, w),
      c0, n0, m0.reshape(b, 1, ML_HEADS))
    return h.reshape(b, w), c, n, m.reshape(b, ML_HEADS)


def _log_sig_pair(z):
    lg = jnp.log(1.0 + jnp.exp(-jnp.abs(z)))
    lb = jnp.minimum(z, 0.0) - lg
    return lb, lb - z


def _suffix_sum(l1m, tri_bf16):
    hi = l1m.astype(BF16)
    lo = (l1m - hi.astype(F32)).astype(BF16)
    return (jnp.dot(hi, tri_bf16, preferred_element_type=F32) +
            jnp.dot(lo, tri_bf16, preferred_element_type=F32))


def _wavefront(n_chains, stages):
    vals = [None] * n_chains
    for w in range(len(stages) + n_chains - 1):
        for c in range(n_chains):
            s = w - c
            if 0 <= s < len(stages):
                vals[c] = stages[s](c, vals[c])
    return vals


def _causal_schedule(nsub):
    return [(j, [(s, s == j) for s in range(j, nsub)]) for j in range(nsub - 1, -1, -1)]


def _sb_prompt_kernel(q_ref, k_ref, v_ref, tri_ref, o_ref, *, bk, nsub):
    base = pl.program_id(2) * nsub
    lane = lax.broadcasted_iota(I32, (bk, LANES), 1)
    tri = tri_ref[...]
    row = lax.broadcasted_iota(I32, (bk, bk), 0)
    colm = lax.broadcasted_iota(I32, (bk, bk), 1)
    causal = colm < row
    qops = []
    for s in range(nsub):
        q = q_ref[s * bk:(s + 1) * bk, :] * (SB_DH ** -0.5)
        qops.append([jnp.where((lane // SB_DH) == h, q, 0.0).astype(BF16) for h in range(2)])

    def run(kb, state, active):
        start = pl.multiple_of(kb * bk, bk)
        kblk = k_ref[pl.ds(start, bk), :].astype(BF16)
        vblk = v_ref[pl.ds(start, bk), :].astype(BF16)
        chains = [(s, h, m) for (s, m) in active for h in range(2)]

        def scores(c, _):
            s, h, _m = chains[c]
            return lax.dot_general(qops[s][h], kblk, (((1,), (1,)), ((), ())), preferred_element_type=F32)

        def logs(c, z):
            lb, l1m = _log_sig_pair(z)
            if chains[c][2]:
                l1m = jnp.where(causal, l1m, 0.0)
            hi = l1m.astype(BF16)
            return lb, l1m, hi, (l1m - hi.astype(F32)).astype(BF16)

        def suffix(c, v):
            lb, l1m, hi, lo = v
            rev = jnp.dot(hi, tri, preferred_element_type=F32) + jnp.dot(lo, tri, preferred_element_type=F32)
            return lb, l1m, rev

        def weights(c, v):
            s, h, m = chains[c]
            lb, l1m, rev = v
            tot = state[s * 2 + h][1]
            a = jnp.exp(lb + rev + tot)
            if m:
                a = jnp.where(causal, a, 0.0)
            return a.astype(BF16), tot + rev[:, 0:1] + l1m[:, 0:1]

        def values(c, v):
            s, h, _m = chains[c]
            a, tot = v
            return state[s * 2 + h][0] + jnp.dot(a, vblk, preferred_element_type=F32), tot

        outs = _wavefront(len(chains), [scores, logs, suffix, weights, values])
        new_state = list(state)
        for c, (s, h, _m) in enumerate(chains):
            new_state[s * 2 + h] = outs[c]
        return tuple(new_state)

    state = tuple((jnp.zeros((bk, LANES), F32), jnp.zeros((bk, 1), F32)) for _ in range(2 * nsub))
    for j, active in _causal_schedule(nsub):
        state = run(base + j, state, active)
    everyone = [(s, False) for s in range(nsub)]
    state = lax.fori_loop(0, base, lambda i, st: run(base - 1 - i, st, everyone), state)
    for s in range(nsub):
        o_ref[s * bk:(s + 1) * bk, :] = jnp.where(lane < SB_DH, state[2 * s][0], state[2 * s + 1][0])


def sb_prompt(sq, sk, sv, batch, seq, bk=256, nsub=2):
    bq = bk * nsub
    nq = seq // bq
    pairs = SB_HEADS // 2
    ri = lax.broadcasted_iota(I32, (bk, bk), 0)
    ci = lax.broadcasted_iota(I32, (bk, bk), 1)
    tri = (ri > ci).astype(BF16)
    return pl.pallas_call(
        functools.partial(_sb_prompt_kernel, bk=bk, nsub=nsub),
        grid=(batch, pairs, nq),
        in_specs=[pl.BlockSpec((bq, LANES), lambda b, p, i: (b * nq + i, p)),
                  pl.BlockSpec((seq, LANES), lambda b, p, i: (b, p)),
                  pl.BlockSpec((seq, LANES), lambda b, p, i: (b, p)),
                  pl.BlockSpec((bk, bk), lambda b, p, i: (0, 0))],
        out_specs=pl.BlockSpec((bq, LANES), lambda b, p, i: (b * nq + i, p)),
        out_shape=jax.ShapeDtypeStruct((batch * seq, SB_HEADS * SB_DH), F32),
        compiler_params=_params(("arbitrary", "arbitrary", "arbitrary")),
        name="sb_prompt",
    )(sq, sk, sv, tri)


def _lane_group_sums(x, ind_bf16):
    n, s, l = x.shape
    x2 = x.reshape(n * s, l)
    hi = x2.astype(BF16)
    lo = (x2 - hi.astype(F32)).astype(BF16)
    y = (jnp.dot(hi, ind_bf16, preferred_element_type=F32) +
         jnp.dot(lo, ind_bf16, preferred_element_type=F32))
    return y.reshape(n, s, ind_bf16.shape[1])


def _sb_decode_kernel(pt_ref, q_ref, *refs, pp):
    k_refs = refs[:pp]
    v_refs = refs[pp:2 * pp]
    tri_ref = refs[2 * pp]
    o_ref = refs[2 * pp + 1]
    acc_sc, tot_sc = refs[2 * pp + 2:]
    p = pl.program_id(1)

    @pl.when(p == 0)
    def _():
        acc_sc[...] = jnp.zeros_like(acc_sc)
        tot_sc[...] = jnp.zeros_like(tot_sc)

    tri = tri_ref[...]
    qcol = [q_ref[0, h] * (SB_DH ** -0.5) for h in range(SB_HEADS)]
    tot = tot_sc[...]
    for i in range(pp):
        z = jnp.concatenate([jnp.sum(k_refs[i][0, h] * qcol[h], axis=0, keepdims=True)
                             for h in range(SB_HEADS)], axis=0)
        lb, l1m = _log_sig_pair(z)
        rev = _suffix_sum(l1m, tri)
        a = jnp.exp(lb + rev + tot)
        tot = tot + rev[:, 0:1] + l1m[:, 0:1]
        for h in range(SB_HEADS):
            acc_sc[h] += v_refs[i][0, h] * a[h:h + 1, :]
    tot_sc[...] = tot

    @pl.when(p == pl.num_programs(1) - 1)
    def _():
        for h in range(SB_HEADS):
            o_ref[0, h] = jnp.sum(acc_sc[h], axis=1, keepdims=True)


def sb_decode(sq, cache_k, cache_v, page_table, pp=4):
    b = sq.shape[0]
    n_pages = page_table.shape[1]
    steps = n_pages // pp
    ck = jnp.transpose(cache_k, (0, 2, 3, 1))
    cv = jnp.transpose(cache_v, (0, 2, 3, 1))
    ri = lax.broadcasted_iota(I32, (PAGE, PAGE), 0)
    ci = lax.broadcasted_iota(I32, (PAGE, PAGE), 1)
    tri = (ri > ci).astype(BF16)

    def page_spec(i):
        return pl.BlockSpec((1, SB_HEADS, SB_DH, PAGE),
                            lambda bi, p, pt: (pt[bi * n_pages + (n_pages - 1 - (p * pp + i))], 0, 0, 0))

    col = pl.BlockSpec((1, SB_HEADS, SB_DH, 1), lambda bi, p, pt: (bi, 0, 0, 0))
    grid_spec = pltpu.PrefetchScalarGridSpec(
        num_scalar_prefetch=1,
        grid=(b, steps),
        in_specs=[col] + [page_spec(i) for i in range(pp)] + [page_spec(i) for i in range(pp)]
                 + [pl.BlockSpec((PAGE, PAGE), lambda bi, p, pt: (0, 0))],
        out_specs=col,
        scratch_shapes=[pltpu.VMEM((SB_HEADS, SB_DH, PAGE), F32), pltpu.VMEM((SB_HEADS, 1), F32)],
    )
    out = pl.pallas_call(
        functools.partial(_sb_decode_kernel, pp=pp),
        grid_spec=grid_spec,
        out_shape=jax.ShapeDtypeStruct((b, SB_HEADS, SB_DH, 1), F32),
        compiler_params=_params(("arbitrary", "arbitrary")),
        name="sb_decode",
    )(page_table.reshape(-1), sq.reshape(b, SB_HEADS, SB_DH, 1), *([ck] * pp), *([cv] * pp), tri)
    return out.reshape(b, SB_HEADS * SB_DH)


def _lambda_value(lam_ref):
    lp = lam_ref[...]
    s1 = jnp.sum(lp[0:1, :] * lp[1:2, :], axis=1, keepdims=True)
    s2 = jnp.sum(lp[2:3, :] * lp[3:4, :], axis=1, keepdims=True)
    return jnp.exp(s1) - jnp.exp(s2) + DF_LAMBDA_INIT


def _diff_prompt_kernel(q_ref, k_ref, v_ref, lam_ref, nw_ref, o_ref, *, bk, nsub):
    base = pl.program_id(2) * nsub
    lane = lax.broadcasted_iota(I32, (bk, LANES), 1)
    row = lax.broadcasted_iota(I32, (bk, bk), 0)
    colm = lax.broadcasted_iota(I32, (bk, bk), 1)
    causal = colm <= row
    qops = []
    for s in range(nsub):
        q = q_ref[s * bk:(s + 1) * bk, :] * (DF_D ** -0.5)
        qops.append([jnp.where((lane // DF_D) == c, q, 0.0).astype(BF16) for c in range(2)])

    def run(kb, state, active):
        start = pl.multiple_of(kb * bk, bk)
        kblk = k_ref[pl.ds(start, bk), :].astype(BF16)
        vblk = v_ref[pl.ds(start, bk), :].astype(BF16)
        chains = [(s, c, m) for (s, m) in active for c in range(2)]

        def scores(i, _):
            s, c, _m = chains[i]
            return lax.dot_general(qops[s][c], kblk, (((1,), (1,)), ((), ())), preferred_element_type=F32)

        def probs(i, z):
            s, c, m = chains[i]
            m_prev, l_prev, _ = state[s * 2 + c]
            if m:
                z = jnp.where(causal, z, NEG_INF)
            m_new = jnp.maximum(m_prev, jnp.max(z, axis=1, keepdims=True))
            p = jnp.exp(z - m_new)
            alpha = jnp.exp(m_prev - m_new)
            return m_new, alpha * l_prev + jnp.sum(p, axis=1, keepdims=True), alpha, p.astype(BF16)

        def values(i, v):
            m_new, l_new, alpha, p = v
            return m_new, l_new, alpha, jnp.dot(p, vblk, preferred_element_type=F32)

        def merge(i, v):
            s, c, _m = chains[i]
            m_new, l_new, alpha, pv = v
            return m_new, l_new, alpha * state[s * 2 + c][2] + pv

        outs = _wavefront(len(chains), [scores, probs, values, merge])
        new_state = list(state)
        for i, (s, c, _m) in enumerate(chains):
            new_state[s * 2 + c] = outs[i]
        return tuple(new_state)

    init = (jnp.full((bk, 1), NEG_INF, F32), jnp.zeros((bk, 1), F32), jnp.zeros((bk, LANES), F32))
    state = (init,) * (2 * nsub)
    for j, active in _causal_schedule(nsub):
        state = run(base + j, state, active)
    everyone = [(s, False) for s in range(nsub)]
    state = lax.fori_loop(0, base, lambda i, st: run(base - 1 - i, st, everyone), state)
    lam = _lambda_value(lam_ref)
    for s in range(nsub):
        (_, l0, a0), (_, l1, a1) = state[2 * s], state[2 * s + 1]
        o = a0 / l0 - lam * (a1 / l1)
        o_ref[s * bk:(s + 1) * bk, :] = _rms(o, nw_ref[...]) * (1.0 - DF_LAMBDA_INIT)


def diff_prompt(q, k, v, lam_pack, norm_w, batch, seq, bk=256, nsub=2):
    bq = bk * nsub
    nq = seq // bq
    return pl.pallas_call(
        functools.partial(_diff_prompt_kernel, bk=bk, nsub=nsub),
        grid=(batch, DF_HEADS, nq),
        in_specs=[pl.BlockSpec((bq, LANES), lambda b, h, i: (b * nq + i, h)),
                  pl.BlockSpec((seq, LANES), lambda b, h, i: (b, h)),
                  pl.BlockSpec((seq, LANES), lambda b, h, i: (b, h)),
                  pl.BlockSpec((8, DF_D), lambda b, h, i: (0, 0)),
                  pl.BlockSpec((1, DF_V), lambda b, h, i: (0, 0))],
        out_specs=pl.BlockSpec((bq, LANES), lambda b, h, i: (b * nq + i, h)),
        out_shape=jax.ShapeDtypeStruct((batch * seq, DF_HEADS * DF_V), F32),
        compiler_params=_params(("arbitrary", "arbitrary", "arbitrary")),
        name="diff_prompt",
    )(q, k, v, lam_pack, norm_w.reshape(1, DF_V))


def _diff_decode_kernel(pt_ref, q_ref, kn_ref, vn_ref, lam_ref, nw_ref, *refs, pp):
    k_refs = refs[:pp]
    v_refs = refs[pp:2 * pp]
    o_ref = refs[2 * pp]
    m_sc, l_sc, acc_sc = refs[2 * pp + 1:]
    p = pl.program_id(1)
    li = lax.broadcasted_iota(I32, (DF_V, 2 * DF_V), 0)
    ni = lax.broadcasted_iota(I32, (DF_V, 2 * DF_V), 1)
    ind = ((li // DF_D) == (ni // DF_V)).astype(BF16)
    q = q_ref[0] * (DF_D ** -0.5)

    @pl.when(p == 0)
    def _():
        z_own = _lane_group_sums((kn_ref[0] * q)[None], ind)[0]
        for c in range(2):
            m_sc[c] = z_own[:, c * DF_V:(c + 1) * DF_V]
            l_sc[c] = jnp.ones((DF_HEADS, DF_V), F32)
            acc_sc[c] = vn_ref[0]

    for i in range(pp):
        z = _lane_group_sums(k_refs[i][0] * q, ind)
        v = v_refs[i][0]
        for c in range(2):
            zc = z[:, :, c * DF_V:(c + 1) * DF_V]
            m_prev = m_sc[c]
            m_new = jnp.maximum(m_prev, jnp.max(zc, axis=0))
            pr = jnp.exp(zc - m_new)
            alpha = jnp.exp(m_prev - m_new)
            l_sc[c] = alpha * l_sc[c] + jnp.sum(pr, axis=0)
            acc_sc[c] = alpha * acc_sc[c] + jnp.sum(pr * v, axis=0)
            m_sc[c] = m_new

    @pl.when(p == pl.num_programs(1) - 1)
    def _():
        lam = _lambda_value(lam_ref)
        o = acc_sc[0] / l_sc[0] - lam * (acc_sc[1] / l_sc[1])
        o_ref[0] = _rms(o, nw_ref[...]) * (1.0 - DF_LAMBDA_INIT)


def diff_decode(q, k_new, v_new, cache_k, cache_v, page_table, lam_pack, norm_w, pp=4):
    b = q.shape[0]
    n_pages = page_table.shape[1]
    steps = n_pages // pp
    hd = (DF_HEADS, DF_V)

    def page_spec(i):
        return pl.BlockSpec((1, PAGE) + hd, lambda bi, p, pt: (pt[bi * n_pages + p * pp + i], 0, 0, 0))

    row = pl.BlockSpec((1,) + hd, lambda bi, p, pt: (bi, 0, 0))
    grid_spec = pltpu.PrefetchScalarGridSpec(
        num_scalar_prefetch=1,
        grid=(b, steps),
        in_specs=[row, row, row,
                  pl.BlockSpec((8, DF_D), lambda bi, p, pt: (0, 0)),
                  pl.BlockSpec((1, DF_V), lambda bi, p, pt: (0, 0))]
                 + [page_spec(i) for i in range(pp)] + [page_spec(i) for i in range(pp)],
        out_specs=row,
        scratch_shapes=[pltpu.VMEM((2,) + hd, F32), pltpu.VMEM((2,) + hd, F32), pltpu.VMEM((2,) + hd, F32)],
    )
    out = pl.pallas_call(
        functools.partial(_diff_decode_kernel, pp=pp),
        grid_spec=grid_spec,
        out_shape=jax.ShapeDtypeStruct((b,) + hd, F32),
        compiler_params=_params(("arbitrary", "arbitrary")),
        name="diff_decode",
    )(page_table.reshape(-1), q.reshape((b,) + hd), k_new.reshape((b,) + hd), v_new.reshape((b,) + hd),
      lam_pack, norm_w.reshape(1, DF_V), *([cache_k] * pp), *([cache_v] * pp))
    return out.reshape(b, DF_HEADS * DF_V)


IDX_NONE = 1e9
HI_HALF = -65536

def _top_rows(tiles, ridx, k):
    n = tiles[0].shape[1]
    kio = lax.broadcasted_iota(I32, (k, n), 0)

    def body(i, carry):
        cur, vals, idxs = carry
        m = cur[0]
        for t in cur[1:]:
            m = jnp.maximum(m, t)
        m = jnp.max(m, axis=0, keepdims=True)
        f = None
        for t, ix in zip(cur, ridx):
            c = jnp.where(t == m, ix, IDX_NONE)
            f = c if f is None else jnp.minimum(f, c)
        f = jnp.min(f, axis=0, keepdims=True)
        cur = tuple(jnp.where(ix == f, NEG_INF, t) for t, ix in zip(cur, ridx))
        vals = jnp.where(kio == i, m, vals)
        idxs = jnp.where(kio == i, f, idxs)
        return cur, vals, idxs

    init = (tuple(tiles), jnp.zeros((k, n), F32), jnp.zeros((k, n), F32))
    _, vals, idxs = lax.fori_loop(0, k, body, init)
    return vals, idxs


def _pair_candidates(s1, s2):
    k, n = s1.shape
    sub = 8
    r8 = lax.broadcasted_iota(I32, (sub, n), 0)
    r8f = r8.astype(F32)
    tiles, ridx = [], []
    for a in range(k):
        nb = k // (a + 1)
        if nb == 1:
            break
        for b0 in range(0, nb, sub):
            t = s1[a:a + 1, :] + s2[b0:b0 + sub, :]
            ix = r8f + float(a * k + b0)
            if b0 + sub > nb:
                keep = r8 < (nb - b0)
                t = jnp.where(keep, t, NEG_INF)
                ix = jnp.where(keep, ix, IDX_NONE)
            tiles.append(t)
            ridx.append(ix)
    a0 = a
    assert (k - a0) % sub == 0
    for a1 in range(a0, k, sub):
        tiles.append(s1[a1:a1 + sub, :] + s2[0:1, :])
        ridx.append((r8f + float(a1)) * float(k))
    return tiles, ridx


def _pick_rows(table, sel):
    kk = table.shape[0]
    out = jnp.zeros(sel.shape, F32)
    for a in range(kk):
        out = jnp.where(sel == float(a), table[a:a + 1, :], out)
    return out


def _peer_select_kernel(h_ref, nw_ref, wq_ref, keys_ref, xn_ref, e_ref, g_ref, e_sc):
    kk = PEER_TOPK
    xn = _rms(h_ref[...], nw_ref[...])
    xn_ref[...] = xn
    q = jnp.dot(xn.astype(BF16), wq_ref[...], preferred_element_type=F32)
    tb = q.shape[0]
    key_idx = [lax.broadcasted_iota(I32, (PEER_NKEYS, tb), 0).astype(F32)]
    for h in range(PEER_HEADS):
        tops = []
        for c in range(2):
            j = h * 2 + c
            s_t = _bdot_nt(keys_ref[c], q[:, j * LANES:(j + 1) * LANES])
            tops.append(_top_rows([s_t], key_idx, kk))
        (s1, i1), (s2, i2) = tops
        c_top, c_idx = _top_rows(*_pair_candidates(s1, s2), kk)
        ia = jnp.floor(c_idx * (1.0 / kk))
        ib = c_idx - ia * kk
        e = _pick_rows(i1, ia) * float(PEER_NKEYS) + _pick_rows(i2, ib)
        ex = jnp.exp(c_top - jnp.max(c_top, axis=0, keepdims=True))
        g = ex / jnp.sum(ex, axis=0, keepdims=True)
        e_sc[h * kk:(h + 1) * kk, :] = e
        for blk in range(tb // LANES):
            g_ref[blk, h * kk:(h + 1) * kk, :] = g[:, blk * LANES:(blk + 1) * LANES]
    e_ref[...] = e_sc[...].T.astype(I32)


def peer_select(h, nw, wq_bf16, keys):
    t, d = h.shape
    tb = 2 * LANES if t % (2 * LANES) == 0 else LANES
    picks = PEER_HEADS * PEER_TOPK
    gb = tb // LANES
    return pl.pallas_call(
        _peer_select_kernel,
        grid=(t // tb,),
        in_specs=[pl.BlockSpec((tb, d), lambda i: (i, 0)),
                  pl.BlockSpec((1, d), lambda i: (0, 0)),
                  pl.BlockSpec(wq_bf16.shape, lambda i: (0, 0)),
                  pl.BlockSpec(keys.shape, lambda i: (0, 0, 0))],
        out_specs=[pl.BlockSpec((tb, d), lambda i: (i, 0)),
                   pl.BlockSpec((tb, picks), lambda i: (i, 0)),
                   pl.BlockSpec((gb, picks, LANES), lambda i: (i, 0, 0))],
        out_shape=[jax.ShapeDtypeStruct((t, d), F32),
                   jax.ShapeDtypeStruct((t, picks), I32),
                   jax.ShapeDtypeStruct((t // LANES, picks, LANES), F32)],
        scratch_shapes=[pltpu.VMEM((picks, tb), F32)],
        compiler_params=_params(("arbitrary",)),
        name="peer_select",
    )(h, nw.reshape(1, d), wq_bf16, keys)


def _peer_gather_kernel(e_hbm, uv_hbm, xn_ref, g_ref, h_ref, o_ref, ids_smem, rows, ids_sem, row_sem,
                        *, n_tok, n_slots):
    i = pl.program_id(0)
    nblk = pl.num_programs(0)
    picks = PEER_HEADS * PEER_TOPK
    d = xn_ref.shape[1]
    tb = xn_ref.shape[0]
    ahead = n_slots - 1
    assert n_tok % n_slots == 0 and n_tok > ahead
    cur = i % 2
    has_next = i + 1 < nblk

    def ids_copy(blk, buf):
        return pltpu.make_async_copy(e_hbm.at[pl.ds(blk * tb, tb)], ids_smem.at[buf], ids_sem.at[buf])

    def issue(buf, t, slot, j0, j1):
        for j in range(j0, j1):
            e = ids_smem[buf, t, j]
            pltpu.make_async_copy(uv_hbm.at[e], rows.at[slot, pl.ds(j, 1)],
                                  row_sem.at[slot]).start(priority=j % 2)

    def wait(slot):
        pltpu.make_async_copy(rows.at[slot], rows.at[slot], row_sem.at[slot]).wait()

    lane = lax.broadcasted_iota(I32, (picks, LANES), 1)
    half = picks // 2

    def token(t, slot, issue_half):
        wait(slot)
        x = xn_ref[pl.ds(t, 1), :]
        u = lax.bitcast_convert_type(rows[slot] & HI_HALF, F32)
        prod = u * x
        part = prod[:, 0:LANES]
        for c in range(1, d // LANES):
            part = part + prod[:, c * LANES:(c + 1) * LANES]
        issue_half(0, half)
        act = jnp.sum(part, axis=1, keepdims=True)
        g = jnp.sum(jnp.where(lane == t, g_ref[0], 0.0), axis=1, keepdims=True)
        wgt = g * jax.nn.gelu(act)
        v = lax.bitcast_convert_type(rows[slot] << 16, F32)
        mix = jnp.sum(wgt * v, axis=0, keepdims=True)
        issue_half(half, picks)
        o_ref[pl.ds(t, 1), :] = h_ref[pl.ds(t, 1), :] + mix

    @pl.when(i == 0)
    def _():
        first = ids_copy(0, 0)
        first.start()
        first.wait()
        for t in range(ahead):
            issue(0, t, t, 0, picks)

    @pl.when(has_next)
    def _():
        ids_copy(i + 1, 1 - cur).start()

    if n_tok < tb:
        o_ref[...] = h_ref[...]

    n_groups = n_tok // n_slots

    def group(gi, c):
        for u in range(n_slots):
            t = gi * n_slots + u
            token(t, u, functools.partial(issue, cur, t + ahead, (u + ahead) % n_slots))
        return c

    lax.fori_loop(0, n_groups - 1, group, 0)

    @pl.when(has_next)
    def _():
        ids_copy(i + 1, 1 - cur).wait()

    for u in range(n_slots):
        t = (n_groups - 1) * n_slots + u
        if u == 0:
            token(t, u, functools.partial(issue, cur, t + ahead, (u + ahead) % n_slots))
        else:
            def issue_next(j0, j1, u=u):
                @pl.when(has_next)
                def _():
                    issue(1 - cur, u - 1, (u + ahead) % n_slots, j0, j1)

            token(t, u, issue_next)


def pack_uv(u, v):
    ub = lax.bitcast_convert_type(u.astype(BF16), jnp.uint16).astype(jnp.uint32)
    vb = lax.bitcast_convert_type(v.astype(BF16), jnp.uint16).astype(jnp.uint32)
    return lax.bitcast_convert_type((ub << 16) | vb, I32)


def peer_gather(e_ids, g_blocks, uv, xn, h, n_tok, nb):
    t, d = h.shape
    tb = LANES
    assert nb == 1 or n_tok == tb
    picks = PEER_HEADS * PEER_TOPK
    n_slots = 16
    return pl.pallas_call(
        functools.partial(_peer_gather_kernel, n_tok=n_tok, n_slots=n_slots),
        grid=(nb,),
        in_specs=[pl.BlockSpec(memory_space=pl.ANY),
                  pl.BlockSpec(memory_space=pl.ANY),
                  pl.BlockSpec((tb, d), lambda i: (i, 0)),
                  pl.BlockSpec((1, picks, tb), lambda i: (i, 0, 0)),
                  pl.BlockSpec((tb, d), lambda i: (i, 0))],
        out_specs=pl.BlockSpec((tb, d), lambda i: (i, 0)),
        out_shape=jax.ShapeDtypeStruct((nb * tb, d), F32),
        scratch_shapes=[pltpu.SMEM((2, tb, picks), I32),
                        pltpu.VMEM((n_slots, picks, d), I32),
                        pltpu.SemaphoreType.DMA((2,)),
                        pltpu.SemaphoreType.DMA((n_slots,))],
        compiler_params=_params(("arbitrary",)),
        name="peer_gather",
    )(e_ids, uv, xn, g_blocks, h)


def peer_layer(h, nw, wq_bf16, keys, uv):
    t, d = h.shape
    tp = -(-t // LANES) * LANES
    hp = h if tp == t else jnp.pad(h, ((0, tp - t), (0, 0)))
    xn, e_ids, g_blocks = peer_select(hp, nw, wq_bf16, keys)
    out = peer_gather(e_ids, g_blocks, uv.reshape(-1, 1, d), xn, hp, min(t, LANES), tp // LANES)
    return out if tp == t else out[:t]


def kernel(x_prompt, x_sample, p_prompt, p_sample, state_mlstm_C, state_mlstm_n, state_mlstm_m, cache_sb_k, cache_sb_v, cache_diff_k, cache_diff_v, page_table, norm_mix0, w_in0, b_igate0, b_fgate0, mlstm_norm0, w_out0, norm_ffn0, peer_wq0, peer_keys0, peer_u0, peer_v0, ple_gate0, ple_proj0, norm_mix1, w_qkv1, lambda_q1, lambda_k1, lambda_q2, lambda_k2, diff_norm1, w_out1, norm_ffn1, peer_wq1, peer_keys1, peer_u1, peer_v1, ple_gate1, ple_proj1, final_norm):
    bp, sp, d = x_prompt.shape
    bs, ss, _ = x_sample.shape
    assert ss == 1
    ml_w = ML_HEADS * ML_DH
    sb_w = SB_HEADS * SB_DH
    df_w = DF_HEADS * DF_V

    g0 = 4 * ml_w
    g1 = g0 + 2 * ML_HEADS
    gates_w = jnp.pad(w_in0[:, g0:g1], ((0, 0), (0, LANES - 2 * ML_HEADS)))
    w_in = jnp.concatenate([w_in0[:, :g0], w_in0[:, g1:], gates_w], axis=1).astype(BF16)
    in_splits = (4 * ml_w, sb_w, sb_w, sb_w, LANES)
    gate_bias = jnp.pad(jnp.concatenate([b_igate0, b_fgate0]), (0, LANES - 2 * ML_HEADS)).reshape(1, LANES)
    w_out0_b = w_out0.astype(BF16)
    w_qkv = w_qkv1.astype(BF16)
    w_out1_b = w_out1.astype(BF16)
    lam_pack = jnp.pad(jnp.stack([lambda_q1, lambda_k1, lambda_q2, lambda_k2]), ((0, 4), (0, 0)))
    uv0 = pack_uv(peer_u0, peer_v0)
    uv1 = pack_uv(peer_u1, peer_v1)
    peer0 = (norm_ffn0, peer_wq0.astype(BF16), peer_keys0, uv0)
    peer1 = (norm_ffn1, peer_wq1.astype(BF16), peer_keys1, uv1)
    ple0 = (ple_gate0.astype(BF16), ple_proj0.astype(BF16))
    ple1 = (ple_gate1.astype(BF16), ple_proj1.astype(BF16))

    def layer0_tail(h, mix_acts, p0):
        h = proj_residual(h, mix_acts, [w_out0_b[:ml_w], w_out0_b[ml_w:]])
        h = peer_layer(h, *peer0)
        return ple(h, p0, *ple0, final_norm, False)

    def layer1_tail(h, o, p1):
        h = proj_residual(h, [o], [w_out1_b])
        h = peer_layer(h, *peer1)
        return ple(h, p1, *ple1, final_norm, True)

    tp = bp * sp
    h = x_prompt.reshape(tp, d)
    ml, sq, sk, sv, gates = norm_proj(h, norm_mix0, w_in, in_splits)
    h_ml, c_p, n_p, m_p = mlstm_prompt(ml, gates, gate_bias, mlstm_norm0, bp, sp)
    h_sb = sb_prompt(sq, sk, sv, bp, sp)
    h = layer0_tail(h, [h_ml, h_sb], p_prompt[0].reshape(tp, -1))
    q1, k1, v1 = norm_proj(h, norm_mix1, w_qkv, (df_w, df_w, df_w))
    o = diff_prompt(q1, k1, v1, lam_pack, diff_norm1, bp, sp)
    y_prompt = layer1_tail(h, o, p_prompt[1].reshape(tp, -1)).reshape(bp, sp, d)

    hs = x_sample.reshape(bs, d)
    ml_s, sq_s, sk_s, sv_s, gates_s = norm_proj(hs, norm_mix0, w_in, in_splits)
    h_ml_s, c_s, n_s, m_s = mlstm_step(ml_s, gates_s, gate_bias, mlstm_norm0,
                                       state_mlstm_C, state_mlstm_n, state_mlstm_m)
    h_sb_s = sb_decode(sq_s, cache_sb_k, cache_sb_v, page_table)
    hs = layer0_tail(hs, [h_ml_s, h_sb_s], p_sample[0].reshape(bs, -1))
    q1s, k1s, v1s = norm_proj(hs, norm_mix1, w_qkv, (df_w, df_w, df_w))
    o_s = diff_decode(q1s, k1s, v1s, cache_diff_k, cache_diff_v, page_table, lam_pack, diff_norm1)
    y_sample = layer1_tail(hs, o_s, p_sample[1].reshape(bs, -1)).reshape(bs, ss, d)

    return (y_prompt, y_sample,
            c_p, n_p, m_p.reshape(bp, ML_HEADS),
            sk.reshape(bp, sp, SB_HEADS, SB_DH), sv.reshape(bp, sp, SB_HEADS, SB_DH),
            k1.reshape(bp, sp, DF_HEADS, DF_V), v1.reshape(bp, sp, DF_HEADS, DF_V),
            c_s, n_s, m_s,
            sk_s.reshape(bs, ss, SB_HEADS, SB_DH), sv_s.reshape(bs, ss, SB_HEADS, SB_DH),
            k1s.reshape(bs, ss, DF_HEADS, DF_V), v1s.reshape(bs, ss, DF_HEADS, DF_V))
```

```python
import functools
import math

import jax
import jax.numpy as jnp
from jax import lax
from jax.experimental import pallas as pl
from jax.experimental.pallas import tpu as pltpu

F32 = jnp.float32
BF16 = jnp.bfloat16
I32 = jnp.int32

RMS_EPS = 1e-6
LANES = 128
ML_HEADS, ML_DH, ML_CHUNK = 4, 128, 64
SB_HEADS, SB_DH = 8, 64
DF_HEADS, DF_D, DF_V = 8, 64, 128
DF_LAMBDA_INIT = 0.8 - 0.6 * math.exp(-0.3 * 1)
PEER_HEADS, PEER_NKEYS, PEER_TOPK = 8, 128, 16
PAGE = 128
VMEM_LIMIT = 56 * 1024 * 1024
NEG_INF = float("-inf")


def _params(sem):
    return pltpu.CompilerParams(dimension_semantics=sem, vmem_limit_bytes=VMEM_LIMIT)


def _bdot(a, b):
    return jnp.dot(a.astype(BF16), b.astype(BF16), preferred_element_type=F32)


def _bdot_nt(a, b):
    return lax.dot_general(a.astype(BF16), b.astype(BF16), (((1,), (1,)), ((), ())),
                           preferred_element_type=F32)


def _rms(x, w):
    return x * lax.rsqrt(jnp.mean(x * x, axis=-1, keepdims=True) + RMS_EPS) * w


def _row_tile(t):
    for tm in (256, 128, 64, 32, 16, 8):
        if t % tm == 0:
            return tm
    raise ValueError(f"row count {t} must be a multiple of 8")


def _norm_proj_kernel(x_ref, nw_ref, w_ref, *o_refs, splits):
    xn = _rms(x_ref[...], nw_ref[...]).astype(BF16)
    off = 0
    for o_ref, n in zip(o_refs, splits):
        o_ref[...] = jnp.dot(xn, w_ref[:, off:off + n], preferred_element_type=F32)
        off += n


def norm_proj(x, nw, w_bf16, splits):
    t, d = x.shape
    n = w_bf16.shape[1]
    assert sum(splits) == n and all(s % LANES == 0 for s in splits)
    tm = _row_tile(t)
    return pl.pallas_call(
        functools.partial(_norm_proj_kernel, splits=tuple(splits)),
        grid=(t // tm,),
        in_specs=[pl.BlockSpec((tm, d), lambda i: (i, 0)),
                  pl.BlockSpec((1, d), lambda i: (0, 0)),
                  pl.BlockSpec((d, n), lambda i: (0, 0))],
        out_specs=[pl.BlockSpec((tm, s), lambda i: (i, 0)) for s in splits],
        out_shape=[jax.ShapeDtypeStruct((t, s), F32) for s in splits],
        compiler_params=_params(("arbitrary",)),
        name="norm_proj",
    )(x, nw.reshape(1, d), w_bf16)


def _proj_res_kernel(*refs, n_in):
    res_ref = refs[0]
    a_refs = refs[1:1 + n_in]
    w_refs = refs[1 + n_in:1 + 2 * n_in]
    o_ref = refs[1 + 2 * n_in]
    acc = res_ref[...]
    for a_ref, w_ref in zip(a_refs, w_refs):
        acc = acc + jnp.dot(a_ref[...].astype(BF16), w_ref[...], preferred_element_type=F32)
    o_ref[...] = acc


def proj_residual(res, acts, ws_bf16):
    t, d = res.shape
    tm = _row_tile(t)
    n_in = len(acts)
    in_specs = [pl.BlockSpec((tm, d), lambda i: (i, 0))]
    in_specs += [pl.BlockSpec((tm, a.shape[1]), lambda i: (i, 0)) for a in acts]
    in_specs += [pl.BlockSpec(w.shape, lambda i: (0, 0)) for w in ws_bf16]
    return pl.pallas_call(
        functools.partial(_proj_res_kernel, n_in=n_in),
        grid=(t // tm,),
        in_specs=in_specs,
        out_specs=pl.BlockSpec((tm, d), lambda i: (i, 0)),
        out_shape=jax.ShapeDtypeStruct((t, d), F32),
        compiler_params=_params(("arbitrary",)),
        name="proj_residual",
    )(res, *acts, *ws_bf16)


def _ple_kernel(h_ref, p_ref, wg_ref, wp_ref, fn_ref, o_ref, *, final_norm):
    h = h_ref[...]
    gate = jax.nn.sigmoid(jnp.dot(h.astype(BF16), wg_ref[...], preferred_element_type=F32))
    emb = jnp.dot(p_ref[...].astype(BF16), wp_ref[...], preferred_element_type=F32)
    y = h + gate * emb
    if final_norm:
        y = _rms(y, fn_ref[...])
    o_ref[...] = y


def ple(h, p, wg_bf16, wp_bf16, fnorm, final_norm):
    t, d = h.shape
    pd = p.shape[1]
    tm = _row_tile(t)
    return pl.pallas_call(
        functools.partial(_ple_kernel, final_norm=final_norm),
        grid=(t // tm,),
        in_specs=[pl.BlockSpec((tm, d), lambda i: (i, 0)),
                  pl.BlockSpec((tm, pd), lambda i: (i, 0)),
                  pl.BlockSpec((d, d), lambda i: (0, 0)),
                  pl.BlockSpec((pd, d), lambda i: (0, 0)),
                  pl.BlockSpec((1, d), lambda i: (0, 0))],
        out_specs=pl.BlockSpec((tm, d), lambda i: (i, 0)),
        out_shape=jax.ShapeDtypeStruct((t, d), F32),
        compiler_params=_params(("arbitrary",)),
        name="ple",
    )(h, p, wg_bf16, wp_bf16, fnorm.reshape(1, d))


def _mlstm_chunk_kernel(q_ref, k_ref, v_ref, o_ref, g_ref, gb_ref, nw_ref,
                        h_ref, c_out, n_out, m_out, c_sc, n_sc, m_sc):
    ci = pl.program_id(1)
    L = ML_CHUNK

    @pl.when(ci == 0)
    def _():
        c_sc[...] = jnp.zeros_like(c_sc)
        n_sc[...] = jnp.zeros_like(n_sc)
        m_sc[...] = jnp.zeros_like(m_sc)

    hp = lax.Precision.HIGHEST
    g = g_ref[...] + gb_ref[...]
    lf = jax.nn.log_sigmoid(g)
    ri = lax.broadcasted_iota(I32, (L, L), 0)
    cj = lax.broadcasted_iota(I32, (L, L), 1)
    tril = (cj <= ri)
    low = tril.astype(F32)
    upp = (ri <= cj).astype(F32)
    eye = (lax.broadcasted_iota(I32, (LANES, LANES), 0) ==
           lax.broadcasted_iota(I32, (LANES, LANES), 1)).astype(F32)
    nt = (((1,), (1,)), ((), ()))
    g_t = lax.dot_general(eye, g, nt, precision=hp, preferred_element_type=F32)
    lf_t = lax.dot_general(eye, lf, nt, precision=hp, preferred_element_type=F32)
    b_c = jnp.dot(low, lf, precision=hp, preferred_element_type=F32)
    b_r = jnp.dot(lf_t, upp, precision=hp, preferred_element_type=F32)

    for h in range(ML_HEADS):
        sl = slice(h * ML_DH, (h + 1) * ML_DH)
        q = q_ref[:, sl] * (ML_DH ** -0.5)
        k = k_ref[:, sl]
        v = v_ref[:, sl]
        li_c = g[:, h:h + 1]
        li_r = g_t[h:h + 1, :]
        bc = b_c[:, ML_HEADS + h:ML_HEADS + h + 1]
        br = b_r[ML_HEADS + h:ML_HEADS + h + 1, :]
        m_prev = m_sc[:, h:h + 1]
        c_prev = c_sc[h]
        n_prev = n_sc[h:h + 1, :]
        dmat = jnp.where(tril, bc - br + li_r, NEG_INF)
        inter = bc + m_prev
        m_t = jnp.maximum(inter, jnp.max(dmat, axis=1, keepdims=True))
        w_inter = jnp.exp(inter - m_t)
        wqk = jnp.exp(dmat - m_t) * _bdot_nt(q, k)
        num = w_inter * _bdot_nt(q, c_prev) + _bdot(wqk, v)
        den = w_inter * jnp.sum(q * n_prev, axis=1, keepdims=True) + jnp.sum(wqk, axis=1, keepdims=True)
        hh = num / jnp.maximum(jnp.abs(den), jnp.exp(-m_t))
        b_last = br[:, L - 1:L]
        g_r = b_last - br + li_r
        g_c = b_last - bc + li_c
        m_new = jnp.maximum(b_last + m_prev, jnp.max(g_r, axis=1, keepdims=True))
        a = jnp.exp(b_last + m_prev - m_new)
        wg_c = jnp.exp(g_c - m_new)
        wv = (wg_c * v).astype(BF16)
        c_new = a * c_prev + lax.dot_general(wv, k.astype(BF16), (((0,), (0,)), ((), ())),
                                             preferred_element_type=F32)
        n_new = a * n_prev + jnp.sum(wg_c * k, axis=0, keepdims=True)
        c_sc[h] = c_new
        n_sc[h:h + 1, :] = n_new
        m_sc[:, h:h + 1] = m_new
        hn = _rms(hh, nw_ref[:, sl])
        h_ref[:, sl] = jax.nn.sigmoid(o_ref[:, sl]) * hn

    @pl.when(ci == pl.num_programs(1) - 1)
    def _():
        c_out[0] = c_sc[...]
        n_out[0] = n_sc[...]
        m_out[0] = m_sc[:, 0:ML_HEADS]


def mlstm_prompt(ml, gates, gate_bias, norm_w, batch, seq):
    L = ML_CHUNK
    nc = seq // L
    w = ML_HEADS * ML_DH
    col = lambda j: pl.BlockSpec((L, w), lambda b, c: (b * nc + c, j))
    return pl.pallas_call(
        _mlstm_chunk_kernel,
        grid=(batch, nc),
        in_specs=[col(0), col(1), col(2), col(3),
                  pl.BlockSpec((L, LANES), lambda b, c: (b * nc + c, 0)),
                  pl.BlockSpec((1, LANES), lambda b, c: (0, 0)),
                  pl.BlockSpec((1, w), lambda b, c: (0, 0))],
        out_specs=[pl.BlockSpec((L, w), lambda b, c: (b * nc + c, 0)),
                   pl.BlockSpec((1, ML_HEADS, ML_DH, ML_DH), lambda b, c: (b, 0, 0, 0)),
                   pl.BlockSpec((1, ML_HEADS, ML_DH), lambda b, c: (b, 0, 0)),
                   pl.BlockSpec((1, 1, ML_HEADS), lambda b, c: (b, 0, 0))],
        out_shape=[jax.ShapeDtypeStruct((batch * seq, w), F32),
                   jax.ShapeDtypeStruct((batch, ML_HEADS, ML_DH, ML_DH), F32),
                   jax.ShapeDtypeStruct((batch, ML_HEADS, ML_DH), F32),
                   jax.ShapeDtypeStruct((batch, 1, ML_HEADS), F32)],
        scratch_shapes=[pltpu.VMEM((ML_HEADS, ML_DH, ML_DH), F32),
                        pltpu.VMEM((ML_HEADS, ML_DH), F32),
                        pltpu.VMEM((1, LANES), F32)],
        compiler_params=_params(("arbitrary", "arbitrary")),
        name="mlstm_chunk",
    )(ml, ml, ml, ml, gates, gate_bias, norm_w.reshape(1, w))


def _mlstm_step_kernel(x_ref, g_ref, gb_ref, nw_ref, c_ref, n_ref, m_ref,
                       h_ref, c_out, n_out, m_out):
    w = ML_HEADS * ML_DH
    g = g_ref[0] + gb_ref[...]
    lf = jax.nn.log_sigmoid(g)
    eye = (lax.broadcasted_iota(I32, (ML_DH, ML_DH), 0) ==
           lax.broadcasted_iota(I32, (ML_DH, ML_DH), 1)).astype(F32)

    def to_col(row):
        return jnp.sum(eye * row, axis=1, keepdims=True)

    def to_row(colv):
        return jnp.sum(eye * colv, axis=0, keepdims=True)

    x = x_ref[0]
    for h in range(ML_HEADS):
        q = x[:, h * ML_DH:(h + 1) * ML_DH] * (ML_DH ** -0.5)
        k = x[:, w + h * ML_DH:w + (h + 1) * ML_DH]
        v = x[:, 2 * w + h * ML_DH:2 * w + (h + 1) * ML_DH]
        o = x[:, 3 * w + h * ML_DH:3 * w + (h + 1) * ML_DH]
        li = g[:, h:h + 1]
        lfh = lf[:, ML_HEADS + h:ML_HEADS + h + 1]
        m_prev = m_ref[0][:, h:h + 1]
        c_prev = c_ref[0, h]
        n_prev = n_ref[0][h:h + 1, :]
        inter = lfh + m_prev
        m_t = jnp.maximum(inter, li)
        w_inter = jnp.exp(inter - m_t)
        wqk = jnp.exp(li - m_t) * jnp.sum(q * k, axis=1, keepdims=True)
        v_col = to_col(v)
        num = w_inter * jnp.sum(c_prev * q, axis=1, keepdims=True) + wqk * v_col
        den = w_inter * jnp.sum(n_prev * q, axis=1, keepdims=True) + wqk
        hh = num / jnp.maximum(jnp.abs(den), jnp.exp(-m_t))
        m_new = m_t
        a = jnp.exp(inter - m_new)
        wg = jnp.exp(li - m_new)
        c_out[0, h] = a * c_prev + (wg * v_col) * k
        n_out[0, h:h + 1, :] = a * n_prev + wg * k
        m_out[0, :, h:h + 1] = m_new
        hr = to_row(hh)
        hn = _rms(hr, nw_ref[:, h * ML_DH:(h + 1) * ML_DH])
        h_ref[0, :, h * ML_DH:(h + 1) * ML_DH] = jax.nn.sigmoid(o) * hn


def mlstm_step(ml, gates, gate_bias, norm_w, c0, n0, m0):
    b = ml.shape[0]
    w = ML_HEADS * ML_DH
    row = lambda n: pl.BlockSpec((1, 1, n), lambda i: (i, 0, 0))
    h, c, n, m = pl.pallas_call(
        _mlstm_step_kernel,
        grid=(b,),
        in_specs=[row(4 * w), row(LANES),
                  pl.BlockSpec((1, LANES), lambda i: (0, 0)),
                  pl.BlockSpec((1, w), lambda i: (0, 0)),
                  pl.BlockSpec((1, ML_HEADS, ML_DH, ML_DH), lambda i: (i, 0, 0, 0)),
                  pl.BlockSpec((1, ML_HEADS, ML_DH), lambda i: (i, 0, 0)),
                  row(ML_HEADS)],
        out_specs=[row(w),
                   pl.BlockSpec((1, ML_HEADS, ML_DH, ML_DH), lambda i: (i, 0, 0, 0)),
                   pl.BlockSpec((1, ML_HEADS, ML_DH), lambda i: (i, 0, 0)),
                   row(ML_HEADS)],
        out_shape=[jax.ShapeDtypeStruct((b, 1, w), F32),
                   jax.ShapeDtypeStruct((b, ML_HEADS, ML_DH, ML_DH), F32),
                   jax.ShapeDtypeStruct((b, ML_HEADS, ML_DH), F32),
                   jax.ShapeDtypeStruct((b, 1, ML_HEADS), F32)],
        compiler_params=_params(("arbitrary",)),
        name="mlstm_step",
    )(ml.reshape(b, 1, 4 * w), gates.reshape(b, 1, LANES), gate_bias, norm_w.reshape(1, w),
      c0, n0, m0.reshape(b, 1, ML_HEADS))
    return h.reshape(b, w), c, n, m.reshape(b, ML_HEADS)


def _log_sig_pair(z):
    lg = jnp.log(1.0 + jnp.exp(-jnp.abs(z)))
    lb = jnp.minimum(z, 0.0) - lg
    return lb, lb - z


def _suffix_sum(l1m, tri_bf16):
    hi = l1m.astype(BF16)
    lo = (l1m - hi.astype(F32)).astype(BF16)
    return (jnp.dot(hi, tri_bf16, preferred_element_type=F32) +
            jnp.dot(lo, tri_bf16, preferred_element_type=F32))


def _wavefront(n_chains, stages):
    vals = [None] * n_chains
    for w in range(len(stages) + n_chains - 1):
        for c in range(n_chains):
            s = w - c
            if 0 <= s < len(stages):
                vals[c] = stages[s](c, vals[c])
    return vals


def _causal_schedule(nsub):
    return [(j, [(s, s == j) for s in range(j, nsub)]) for j in range(nsub - 1, -1, -1)]


def _sb_prompt_kernel(q_ref, k_ref, v_ref, tri_ref, o_ref, *, bk, nsub):
    base = pl.program_id(2) * nsub
    lane = lax.broadcasted_iota(I32, (bk, LANES), 1)
    tri = tri_ref[...]
    row = lax.broadcasted_iota(I32, (bk, bk), 0)
    colm = lax.broadcasted_iota(I32, (bk, bk), 1)
    causal = colm < row
    qops = []
    for s in range(nsub):
        q = q_ref[s * bk:(s + 1) * bk, :] * (SB_DH ** -0.5)
        qops.append([jnp.where((lane // SB_DH) == h, q, 0.0).astype(BF16) for h in range(2)])

    def run(kb, state, active):
        start = pl.multiple_of(kb * bk, bk)
        kblk = k_ref[pl.ds(start, bk), :].astype(BF16)
        vblk = v_ref[pl.ds(start, bk), :].astype(BF16)
        chains = [(s, h, m) for (s, m) in active for h in range(2)]

        def scores(c, _):
            s, h, _m = chains[c]
            return lax.dot_general(qops[s][h], kblk, (((1,), (1,)), ((), ())), preferred_element_type=F32)

        def logs(c, z):
            lb, l1m = _log_sig_pair(z)
            if chains[c][2]:
                l1m = jnp.where(causal, l1m, 0.0)
            hi = l1m.astype(BF16)
            return lb, l1m, hi, (l1m - hi.astype(F32)).astype(BF16)

        def suffix(c, v):
            lb, l1m, hi, lo = v
            rev = jnp.dot(hi, tri, preferred_element_type=F32) + jnp.dot(lo, tri, preferred_element_type=F32)
            return lb, l1m, rev

        def weights(c, v):
            s, h, m = chains[c]
            lb, l1m, rev = v
            tot = state[s * 2 + h][1]
            a = jnp.exp(lb + rev + tot)
            if m:
                a = jnp.where(causal, a, 0.0)
            return a.astype(BF16), tot + rev[:, 0:1] + l1m[:, 0:1]

        def values(c, v):
            s, h, _m = chains[c]
            a, tot = v
            return state[s * 2 + h][0] + jnp.dot(a, vblk, preferred_element_type=F32), tot

        outs = _wavefront(len(chains), [scores, logs, suffix, weights, values])
        new_state = list(state)
        for c, (s, h, _m) in enumerate(chains):
            new_state[s * 2 + h] = outs[c]
        return tuple(new_state)

    state = tuple((jnp.zeros((bk, LANES), F32), jnp.zeros((bk, 1), F32)) for _ in range(2 * nsub))
    for j, active in _causal_schedule(nsub):
        state = run(base + j, state, active)
    everyone = [(s, False) for s in range(nsub)]
    state = lax.fori_loop(0, base, lambda i, st: run(base - 1 - i, st, everyone), state)
    for s in range(nsub):
        o_ref[s * bk:(s + 1) * bk, :] = jnp.where(lane < SB_DH, state[2 * s][0], state[2 * s + 1][0])


def sb_prompt(sq, sk, sv, batch, seq, bk=256, nsub=4):
    bq = bk * nsub
    nq = seq // bq
    pairs = SB_HEADS // 2
    ri = lax.broadcasted_iota(I32, (bk, bk), 0)
    ci = lax.broadcasted_iota(I32, (bk, bk), 1)
    tri = (ri > ci).astype(BF16)
    return pl.pallas_call(
        functools.partial(_sb_prompt_kernel, bk=bk, nsub=nsub),
        grid=(batch, pairs, nq),
        in_specs=[pl.BlockSpec((bq, LANES), lambda b, p, i: (b * nq + i, p)),
                  pl.BlockSpec((seq, LANES), lambda b, p, i: (b, p)),
                  pl.BlockSpec((seq, LANES), lambda b, p, i: (b, p)),
                  pl.BlockSpec((bk, bk), lambda b, p, i: (0, 0))],
        out_specs=pl.BlockSpec((bq, LANES), lambda b, p, i: (b * nq + i, p)),
        out_shape=jax.ShapeDtypeStruct((batch * seq, SB_HEADS * SB_DH), F32),
        compiler_params=_params(("arbitrary", "arbitrary", "arbitrary")),
        name="sb_prompt",
    )(sq, sk, sv, tri)


def _lane_group_sums(x, ind_bf16):
    n, s, l = x.shape
    x2 = x.reshape(n * s, l)
    hi = x2.astype(BF16)
    lo = (x2 - hi.astype(F32)).astype(BF16)
    y = (jnp.dot(hi, ind_bf16, preferred_element_type=F32) +
         jnp.dot(lo, ind_bf16, preferred_element_type=F32))
    return y.reshape(n, s, ind_bf16.shape[1])


def _sb_decode_kernel(pt_ref, q_ref, *refs, pp):
    k_refs = refs[:pp]
    v_refs = refs[pp:2 * pp]
    tri_ref = refs[2 * pp]
    o_ref = refs[2 * pp + 1]
    acc_sc, tot_sc = refs[2 * pp + 2:]
    p = pl.program_id(1)

    @pl.when(p == 0)
    def _():
        acc_sc[...] = jnp.zeros_like(acc_sc)
        tot_sc[...] = jnp.zeros_like(tot_sc)

    tri = tri_ref[...]
    qcol = [q_ref[0, h] * (SB_DH ** -0.5) for h in range(SB_HEADS)]
    tot = tot_sc[...]
    for i in range(pp):
        z = jnp.concatenate([jnp.sum(k_refs[i][0, h] * qcol[h], axis=0, keepdims=True)
                             for h in range(SB_HEADS)], axis=0)
        lb, l1m = _log_sig_pair(z)
        rev = _suffix_sum(l1m, tri)
        a = jnp.exp(lb + rev + tot)
        tot = tot + rev[:, 0:1] + l1m[:, 0:1]
        for h in range(SB_HEADS):
            acc_sc[h] += v_refs[i][0, h] * a[h:h + 1, :]
    tot_sc[...] = tot

    @pl.when(p == pl.num_programs(1) - 1)
    def _():
        for h in range(SB_HEADS):
            o_ref[0, h] = jnp.sum(acc_sc[h], axis=1, keepdims=True)


def sb_decode(sq, cache_k, cache_v, page_table, pp=4):
    b = sq.shape[0]
    n_pages = page_table.shape[1]
    steps = n_pages // pp
    ck = jnp.transpose(cache_k, (0, 2, 3, 1))
    cv = jnp.transpose(cache_v, (0, 2, 3, 1))
    ri = lax.broadcasted_iota(I32, (PAGE, PAGE), 0)
    ci = lax.broadcasted_iota(I32, (PAGE, PAGE), 1)
    tri = (ri > ci).astype(BF16)

    def page_spec(i):
        return pl.BlockSpec((1, SB_HEADS, SB_DH, PAGE),
                            lambda bi, p, pt: (pt[bi * n_pages + (n_pages - 1 - (p * pp + i))], 0, 0, 0))

    col = pl.BlockSpec((1, SB_HEADS, SB_DH, 1), lambda bi, p, pt: (bi, 0, 0, 0))
    grid_spec = pltpu.PrefetchScalarGridSpec(
        num_scalar_prefetch=1,
        grid=(b, steps),
        in_specs=[col] + [page_spec(i) for i in range(pp)] + [page_spec(i) for i in range(pp)]
                 + [pl.BlockSpec((PAGE, PAGE), lambda bi, p, pt: (0, 0))],
        out_specs=col,
        scratch_shapes=[pltpu.VMEM((SB_HEADS, SB_DH, PAGE), F32), pltpu.VMEM((SB_HEADS, 1), F32)],
    )
    out = pl.pallas_call(
        functools.partial(_sb_decode_kernel, pp=pp),
        grid_spec=grid_spec,
        out_shape=jax.ShapeDtypeStruct((b, SB_HEADS, SB_DH, 1), F32),
        compiler_params=_params(("arbitrary", "arbitrary")),
        name="sb_decode",
    )(page_table.reshape(-1), sq.reshape(b, SB_HEADS, SB_DH, 1), *([ck] * pp), *([cv] * pp), tri)
    return out.reshape(b, SB_HEADS * SB_DH)


def _lambda_value(lam_ref):
    lp = lam_ref[...]
    s1 = jnp.sum(lp[0:1, :] * lp[1:2, :], axis=1, keepdims=True)
    s2 = jnp.sum(lp[2:3, :] * lp[3:4, :], axis=1, keepdims=True)
    return jnp.exp(s1) - jnp.exp(s2) + DF_LAMBDA_INIT


def _diff_prompt_kernel(q_ref, k_ref, v_ref, lam_ref, nw_ref, o_ref, *, bk, nsub):
    base = pl.program_id(2) * nsub
    lane = lax.broadcasted_iota(I32, (bk, LANES), 1)
    row = lax.broadcasted_iota(I32, (bk, bk), 0)
    colm = lax.broadcasted_iota(I32, (bk, bk), 1)
    causal = colm <= row
    qops = []
    for s in range(nsub):
        q = q_ref[s * bk:(s + 1) * bk, :] * (DF_D ** -0.5)
        qops.append([jnp.where((lane // DF_D) == c, q, 0.0).astype(BF16) for c in range(2)])

    def run(kb, state, active):
        start = pl.multiple_of(kb * bk, bk)
        kblk = k_ref[pl.ds(start, bk), :].astype(BF16)
        vblk = v_ref[pl.ds(start, bk), :].astype(BF16)
        chains = [(s, c, m) for (s, m) in active for c in range(2)]

        def scores(i, _):
            s, c, _m = chains[i]
            return lax.dot_general(qops[s][c], kblk, (((1,), (1,)), ((), ())), preferred_element_type=F32)

        def probs(i, z):
            s, c, m = chains[i]
            m_prev, l_prev, _ = state[s * 2 + c]
            if m:
                z = jnp.where(causal, z, NEG_INF)
            m_new = jnp.maximum(m_prev, jnp.max(z, axis=1, keepdims=True))
            p = jnp.exp(z - m_new)
            alpha = jnp.exp(m_prev - m_new)
            return m_new, alpha * l_prev + jnp.sum(p, axis=1, keepdims=True), alpha, p.astype(BF16)

        def values(i, v):
            m_new, l_new, alpha, p = v
            return m_new, l_new, alpha, jnp.dot(p, vblk, preferred_element_type=F32)

        def merge(i, v):
            s, c, _m = chains[i]
            m_new, l_new, alpha, pv = v
            return m_new, l_new, alpha * state[s * 2 + c][2] + pv

        outs = _wavefront(len(chains), [scores, probs, values, merge])
        new_state = list(state)
        for i, (s, c, _m) in enumerate(chains):
            new_state[s * 2 + c] = outs[i]
        return tuple(new_state)

    init = (jnp.full((bk, 1), NEG_INF, F32), jnp.zeros((bk, 1), F32), jnp.zeros((bk, LANES), F32))
    state = (init,) * (2 * nsub)
    for j, active in _causal_schedule(nsub):
        state = run(base + j, state, active)
    everyone = [(s, False) for s in range(nsub)]
    state = lax.fori_loop(0, base, lambda i, st: run(base - 1 - i, st, everyone), state)
    lam = _lambda_value(lam_ref)
    for s in range(nsub):
        (_, l0, a0), (_, l1, a1) = state[2 * s], state[2 * s + 1]
        o = a0 / l0 - lam * (a1 / l1)
        o_ref[s * bk:(s + 1) * bk, :] = _rms(o, nw_ref[...]) * (1.0 - DF_LAMBDA_INIT)


def diff_prompt(q, k, v, lam_pack, norm_w, batch, seq, bk=256, nsub=4):
    bq = bk * nsub
    nq = seq // bq
    return pl.pallas_call(
        functools.partial(_diff_prompt_kernel, bk=bk, nsub=nsub),
        grid=(batch, DF_HEADS, nq),
        in_specs=[pl.BlockSpec((bq, LANES), lambda b, h, i: (b * nq + i, h)),
                  pl.BlockSpec((seq, LANES), lambda b, h, i: (b, h)),
                  pl.BlockSpec((seq, LANES), lambda b, h, i: (b, h)),
                  pl.BlockSpec((8, DF_D), lambda b, h, i: (0, 0)),
                  pl.BlockSpec((1, DF_V), lambda b, h, i: (0, 0))],
        out_specs=pl.BlockSpec((bq, LANES), lambda b, h, i: (b * nq + i, h)),
        out_shape=jax.ShapeDtypeStruct((batch * seq, DF_HEADS * DF_V), F32),
        compiler_params=_params(("arbitrary", "arbitrary", "arbitrary")),
        name="diff_prompt",
    )(q, k, v, lam_pack, norm_w.reshape(1, DF_V))


def _diff_decode_kernel(pt_ref, q_ref, kn_ref, vn_ref, lam_ref, nw_ref, *refs, pp):
    k_refs = refs[:pp]
    v_refs = refs[pp:2 * pp]
    o_ref = refs[2 * pp]
    m_sc, l_sc, acc_sc = refs[2 * pp + 1:]
    p = pl.program_id(1)
    li = lax.broadcasted_iota(I32, (DF_V, 2 * DF_V), 0)
    ni = lax.broadcasted_iota(I32, (DF_V, 2 * DF_V), 1)
    ind = ((li // DF_D) == (ni // DF_V)).astype(BF16)
    q = q_ref[0] * (DF_D ** -0.5)

    @pl.when(p == 0)
    def _():
        z_own = _lane_group_sums((kn_ref[0] * q)[None], ind)[0]
        for c in range(2):
            m_sc[c] = z_own[:, c * DF_V:(c + 1) * DF_V]
            l_sc[c] = jnp.ones((DF_HEADS, DF_V), F32)
            acc_sc[c] = vn_ref[0]

    for i in range(pp):
        z = _lane_group_sums(k_refs[i][0] * q, ind)
        v = v_refs[i][0]
        for c in range(2):
            zc = z[:, :, c * DF_V:(c + 1) * DF_V]
            m_prev = m_sc[c]
            m_new = jnp.maximum(m_prev, jnp.max(zc, axis=0))
            pr = jnp.exp(zc - m_new)
            alpha = jnp.exp(m_prev - m_new)
            l_sc[c] = alpha * l_sc[c] + jnp.sum(pr, axis=0)
            acc_sc[c] = alpha * acc_sc[c] + jnp.sum(pr * v, axis=0)
            m_sc[c] = m_new

    @pl.when(p == pl.num_programs(1) - 1)
    def _():
        lam = _lambda_value(lam_ref)
        o = acc_sc[0] / l_sc[0] - lam * (acc_sc[1] / l_sc[1])
        o_ref[0] = _rms(o, nw_ref[...]) * (1.0 - DF_LAMBDA_INIT)


def diff_decode(q, k_new, v_new, cache_k, cache_v, page_table, lam_pack, norm_w, pp=4):
    b = q.shape[0]
    n_pages = page_table.shape[1]
    steps = n_pages // pp
    hd = (DF_HEADS, DF_V)

    def page_spec(i):
        return pl.BlockSpec((1, PAGE) + hd, lambda bi, p, pt: (pt[bi * n_pages + p * pp + i], 0, 0, 0))

    row = pl.BlockSpec((1,) + hd, lambda bi, p, pt: (bi, 0, 0))
    grid_spec = pltpu.PrefetchScalarGridSpec(
        num_scalar_prefetch=1,
        grid=(b, steps),
        in_specs=[row, row, row,
                  pl.BlockSpec((8, DF_D), lambda bi, p, pt: (0, 0)),
                  pl.BlockSpec((1, DF_V), lambda bi, p, pt: (0, 0))]
                 + [page_spec(i) for i in range(pp)] + [page_spec(i) for i in range(pp)],
        out_specs=row,
        scratch_shapes=[pltpu.VMEM((2,) + hd, F32), pltpu.VMEM((2,) + hd, F32), pltpu.VMEM((2,) + hd, F32)],
    )
    out = pl.pallas_call(
        functools.partial(_diff_decode_kernel, pp=pp),
        grid_spec=grid_spec,
        out_shape=jax.ShapeDtypeStruct((b,) + hd, F32),
        compiler_params=_params(("arbitrary", "arbitrary")),
        name="diff_decode",
    )(page_table.reshape(-1), q.reshape((b,) + hd), k_new.reshape((b,) + hd), v_new.reshape((b,) + hd),
      lam_pack, norm_w.reshape(1, DF_V), *([cache_k] * pp), *([cache_v] * pp))
    return out.reshape(b, DF_HEADS * DF_V)


IDX_NONE = 1e9
HI_HALF = -65536

def _top_rows(tiles, ridx, k):
    n = tiles[0].shape[1]
    kio = lax.broadcasted_iota(I32, (k, n), 0)

    def body(i, carry):
        cur, vals, idxs = carry
        m = cur[0]
        for t in cur[1:]:
            m = jnp.maximum(m, t)
        m = jnp.max(m, axis=0, keepdims=True)
        f = None
        for t, ix in zip(cur, ridx):
            c = jnp.where(t == m, ix, IDX_NONE)
            f = c if f is None else jnp.minimum(f, c)
        f = jnp.min(f, axis=0, keepdims=True)
        cur = tuple(jnp.where(ix == f, NEG_INF, t) for t, ix in zip(cur, ridx))
        vals = jnp.where(kio == i, m, vals)
        idxs = jnp.where(kio == i, f, idxs)
        return cur, vals, idxs

    init = (tuple(tiles), jnp.zeros((k, n), F32), jnp.zeros((k, n), F32))
    _, vals, idxs = lax.fori_loop(0, k, body, init)
    return vals, idxs


def _pair_candidates(s1, s2):
    k, n = s1.shape
    sub = 8
    r8 = lax.broadcasted_iota(I32, (sub, n), 0)
    r8f = r8.astype(F32)
    tiles, ridx = [], []
    for a in range(k):
        nb = k // (a + 1)
        if nb == 1:
            break
        for b0 in range(0, nb, sub):
            t = s1[a:a + 1, :] + s2[b0:b0 + sub, :]
            ix = r8f + float(a * k + b0)
            if b0 + sub > nb:
                keep = r8 < (nb - b0)
                t = jnp.where(keep, t, NEG_INF)
                ix = jnp.where(keep, ix, IDX_NONE)
            tiles.append(t)
            ridx.append(ix)
    a0 = a
    assert (k - a0) % sub == 0
    for a1 in range(a0, k, sub):
        tiles.append(s1[a1:a1 + sub, :] + s2[0:1, :])
        ridx.append((r8f + float(a1)) * float(k))
    return tiles, ridx


def _pick_rows(table, sel):
    kk = table.shape[0]
    out = jnp.zeros(sel.shape, F32)
    for a in range(kk):
        out = jnp.where(sel == float(a), table[a:a + 1, :], out)
    return out


def _peer_select_kernel(h_ref, nw_ref, wq_ref, keys_ref, xn_ref, e_ref, g_ref, e_sc):
    kk = PEER_TOPK
    xn = _rms(h_ref[...], nw_ref[...])
    xn_ref[...] = xn
    q = jnp.dot(xn.astype(BF16), wq_ref[...], preferred_element_type=F32)
    tb = q.shape[0]
    key_idx = [lax.broadcasted_iota(I32, (PEER_NKEYS, tb), 0).astype(F32)]
    for h in range(PEER_HEADS):
        tops = []
        for c in range(2):
            j = h * 2 + c
            s_t = _bdot_nt(keys_ref[c], q[:, j * LANES:(j + 1) * LANES])
            tops.append(_top_rows([s_t], key_idx, kk))
        (s1, i1), (s2, i2) = tops
        c_top, c_idx = _top_rows(*_pair_candidates(s1, s2), kk)
        ia = jnp.floor(c_idx * (1.0 / kk))
        ib = c_idx - ia * kk
        e = _pick_rows(i1, ia) * float(PEER_NKEYS) + _pick_rows(i2, ib)
        ex = jnp.exp(c_top - jnp.max(c_top, axis=0, keepdims=True))
        g = ex / jnp.sum(ex, axis=0, keepdims=True)
        e_sc[h * kk:(h + 1) * kk, :] = e
        for blk in range(tb // LANES):
            g_ref[blk, h * kk:(h + 1) * kk, :] = g[:, blk * LANES:(blk + 1) * LANES]
    e_ref[...] = e_sc[...].T.astype(I32)


def peer_select(h, nw, wq_bf16, keys):
    t, d = h.shape
    tb = 2 * LANES if t % (2 * LANES) == 0 else LANES
    picks = PEER_HEADS * PEER_TOPK
    gb = tb // LANES
    return pl.pallas_call(
        _peer_select_kernel,
        grid=(t // tb,),
        in_specs=[pl.BlockSpec((tb, d), lambda i: (i, 0)),
                  pl.BlockSpec((1, d), lambda i: (0, 0)),
                  pl.BlockSpec(wq_bf16.shape, lambda i: (0, 0)),
                  pl.BlockSpec(keys.shape, lambda i: (0, 0, 0))],
        out_specs=[pl.BlockSpec((tb, d), lambda i: (i, 0)),
                   pl.BlockSpec((tb, picks), lambda i: (i, 0)),
                   pl.BlockSpec((gb, picks, LANES), lambda i: (i, 0, 0))],
        out_shape=[jax.ShapeDtypeStruct((t, d), F32),
                   jax.ShapeDtypeStruct((t, picks), I32),
                   jax.ShapeDtypeStruct((t // LANES, picks, LANES), F32)],
        scratch_shapes=[pltpu.VMEM((picks, tb), F32)],
        compiler_params=_params(("arbitrary",)),
        name="peer_select",
    )(h, nw.reshape(1, d), wq_bf16, keys)


def _peer_gather_kernel(e_hbm, uv_hbm, xn_ref, g_ref, h_ref, o_ref, ids_smem, rows, ids_sem, row_sem,
                        *, n_tok, n_slots):
    i = pl.program_id(0)
    nblk = pl.num_programs(0)
    picks = PEER_HEADS * PEER_TOPK
    d = xn_ref.shape[1]
    tb = xn_ref.shape[0]
    ahead = n_slots - 1
    assert n_tok % n_slots == 0 and n_tok > ahead
    cur = i % 2
    has_next = i + 1 < nblk

    def ids_copy(blk, buf):
        return pltpu.make_async_copy(e_hbm.at[pl.ds(blk * tb, tb)], ids_smem.at[buf], ids_sem.at[buf])

    def issue(buf, t, slot, j0, j1):
        for j in range(j0, j1):
            e = ids_smem[buf, t, j]
            pltpu.make_async_copy(uv_hbm.at[e], rows.at[slot, pl.ds(j, 1)],
                                  row_sem.at[slot]).start(priority=j % 2)

    def wait(slot):
        pltpu.make_async_copy(rows.at[slot], rows.at[slot], row_sem.at[slot]).wait()

    lane = lax.broadcasted_iota(I32, (picks, LANES), 1)
    half = picks // 2

    def token(t, slot, issue_half):
        wait(slot)
        x = xn_ref[pl.ds(t, 1), :]
        u = lax.bitcast_convert_type(rows[slot] & HI_HALF, F32)
        prod = u * x
        part = prod[:, 0:LANES]
        for c in range(1, d // LANES):
            part = part + prod[:, c * LANES:(c + 1) * LANES]
        issue_half(0, half)
        act = jnp.sum(part, axis=1, keepdims=True)
        g = jnp.sum(jnp.where(lane == t, g_ref[0], 0.0), axis=1, keepdims=True)
        wgt = g * jax.nn.gelu(act)
        v = lax.bitcast_convert_type(rows[slot] << 16, F32)
        mix = jnp.sum(wgt * v, axis=0, keepdims=True)
        issue_half(half, picks)
        o_ref[pl.ds(t, 1), :] = h_ref[pl.ds(t, 1), :] + mix

    @pl.when(i == 0)
    def _():
        first = ids_copy(0, 0)
        first.start()
        first.wait()
        for t in range(ahead):
            issue(0, t, t, 0, picks)

    @pl.when(has_next)
    def _():
        ids_copy(i + 1, 1 - cur).start()

    if n_tok < tb:
        o_ref[...] = h_ref[...]

    n_groups = n_tok // n_slots

    def group(gi, c):
        for u in range(n_slots):
            t = gi * n_slots + u
            token(t, u, functools.partial(issue, cur, t + ahead, (u + ahead) % n_slots))
        return c

    lax.fori_loop(0, n_groups - 1, group, 0)

    @pl.when(has_next)
    def _():
        ids_copy(i + 1, 1 - cur).wait()

    for u in range(n_slots):
        t = (n_groups - 1) * n_slots + u
        if u == 0:
            token(t, u, functools.partial(issue, cur, t + ahead, (u + ahead) % n_slots))
        else:
            def issue_next(j0, j1, u=u):
                @pl.when(has_next)
                def _():
                    issue(1 - cur, u - 1, (u + ahead) % n_slots, j0, j1)

            token(t, u, issue_next)


def pack_uv(u, v):
    ub = lax.bitcast_convert_type(u.astype(BF16), jnp.uint16).astype(jnp.uint32)
    vb = lax.bitcast_convert_type(v.astype(BF16), jnp.uint16).astype(jnp.uint32)
    return lax.bitcast_convert_type((ub << 16) | vb, I32)


def peer_gather(e_ids, g_blocks, uv, xn, h, n_tok, nb):
    t, d = h.shape
    tb = LANES
    assert nb == 1 or n_tok == tb
    picks = PEER_HEADS * PEER_TOPK
    n_slots = 16
    return pl.pallas_call(
        functools.partial(_peer_gather_kernel, n_tok=n_tok, n_slots=n_slots),
        grid=(nb,),
        in_specs=[pl.BlockSpec(memory_space=pl.ANY),
                  pl.BlockSpec(memory_space=pl.ANY),
                  pl.BlockSpec((tb, d), lambda i: (i, 0)),
                  pl.BlockSpec((1, picks, tb), lambda i: (i, 0, 0)),
                  pl.BlockSpec((tb, d), lambda i: (i, 0))],
        out_specs=pl.BlockSpec((tb, d), lambda i: (i, 0)),
        out_shape=jax.ShapeDtypeStruct((nb * tb, d), F32),
        scratch_shapes=[pltpu.SMEM((2, tb, picks), I32),
                        pltpu.VMEM((n_slots, picks, d), I32),
                        pltpu.SemaphoreType.DMA((2,)),
                        pltpu.SemaphoreType.DMA((n_slots,))],
        compiler_params=_params(("arbitrary",)),
        name="peer_gather",
    )(e_ids, uv, xn, g_blocks, h)


def peer_layer(h, nw, wq_bf16, keys, uv):
    t, d = h.shape
    tp = -(-t // LANES) * LANES
    hp = h if tp == t else jnp.pad(h, ((0, tp - t), (0, 0)))
    xn, e_ids, g_blocks = peer_select(hp, nw, wq_bf16, keys)
    out = peer_gather(e_ids, g_blocks, uv.reshape(-1, 1, d), xn, hp, min(t, LANES), tp // LANES)
    return out if tp == t else out[:t]


def kernel(x_prompt, x_sample, p_prompt, p_sample, state_mlstm_C, state_mlstm_n, state_mlstm_m, cache_sb_k, cache_sb_v, cache_diff_k, cache_diff_v, page_table, norm_mix0, w_in0, b_igate0, b_fgate0, mlstm_norm0, w_out0, norm_ffn0, peer_wq0, peer_keys0, peer_u0, peer_v0, ple_gate0, ple_proj0, norm_mix1, w_qkv1, lambda_q1, lambda_k1, lambda_q2, lambda_k2, diff_norm1, w_out1, norm_ffn1, peer_wq1, peer_keys1, peer_u1, peer_v1, ple_gate1, ple_proj1, final_norm):
    bp, sp, d = x_prompt.shape
    bs, ss, _ = x_sample.shape
    assert ss == 1
    ml_w = ML_HEADS * ML_DH
    sb_w = SB_HEADS * SB_DH
    df_w = DF_HEADS * DF_V

    g0 = 4 * ml_w
    g1 = g0 + 2 * ML_HEADS
    gates_w = jnp.pad(w_in0[:, g0:g1], ((0, 0), (0, LANES - 2 * ML_HEADS)))
    w_in = jnp.concatenate([w_in0[:, :g0], w_in0[:, g1:], gates_w], axis=1).astype(BF16)
    in_splits = (4 * ml_w, sb_w, sb_w, sb_w, LANES)
    gate_bias = jnp.pad(jnp.concatenate([b_igate0, b_fgate0]), (0, LANES - 2 * ML_HEADS)).reshape(1, LANES)
    w_out0_b = w_out0.astype(BF16)
    w_qkv = w_qkv1.astype(BF16)
    w_out1_b = w_out1.astype(BF16)
    lam_pack = jnp.pad(jnp.stack([lambda_q1, lambda_k1, lambda_q2, lambda_k2]), ((0, 4), (0, 0)))
    uv0 = pack_uv(peer_u0, peer_v0)
    uv1 = pack_uv(peer_u1, peer_v1)
    peer0 = (norm_ffn0, peer_wq0.astype(BF16), peer_keys0, uv0)
    peer1 = (norm_ffn1, peer_wq1.astype(BF16), peer_keys1, uv1)
    ple0 = (ple_gate0.astype(BF16), ple_proj0.astype(BF16))
    ple1 = (ple_gate1.astype(BF16), ple_proj1.astype(BF16))

    def layer0_tail(h, mix_acts, p0):
        h = proj_residual(h, mix_acts, [w_out0_b[:ml_w], w_out0_b[ml_w:]])
        h = peer_layer(h, *peer0)
        return ple(h, p0, *ple0, final_norm, False)

    def layer1_tail(h, o, p1):
        h = proj_residual(h, [o], [w_out1_b])
        h = peer_layer(h, *peer1)
        return ple(h, p1, *ple1, final_norm, True)

    tp = bp * sp
    h = x_prompt.reshape(tp, d)
    ml, sq, sk, sv, gates = norm_proj(h, norm_mix0, w_in, in_splits)
    h_ml, c_p, n_p, m_p = mlstm_prompt(ml, gates, gate_bias, mlstm_norm0, bp, sp)
    h_sb = sb_prompt(sq, sk, sv, bp, sp)
    h = layer0_tail(h, [h_ml, h_sb], p_prompt[0].reshape(tp, -1))
    q1, k1, v1 = norm_proj(h, norm_mix1, w_qkv, (df_w, df_w, df_w))
    o = diff_prompt(q1, k1, v1, lam_pack, diff_norm1, bp, sp)
    y_prompt = layer1_tail(h, o, p_prompt[1].reshape(tp, -1)).reshape(bp, sp, d)

    hs = x_sample.reshape(bs, d)
    ml_s, sq_s, sk_s, sv_s, gates_s = norm_proj(hs, norm_mix0, w_in, in_splits)
    h_ml_s, c_s, n_s, m_s = mlstm_step(ml_s, gates_s, gate_bias, mlstm_norm0,
                                       state_mlstm_C, state_mlstm_n, state_mlstm_m)
    h_sb_s = sb_decode(sq_s, cache_sb_k, cache_sb_v, page_table)
    hs = layer0_tail(hs, [h_ml_s, h_sb_s], p_sample[0].reshape(bs, -1))
    q1s, k1s, v1s = norm_proj(hs, norm_mix1, w_qkv, (df_w, df_w, df_w))
    o_s = diff_decode(q1s, k1s, v1s, cache_diff_k, cache_diff_v, page_table, lam_pack, diff_norm1)
    y_sample = layer1_tail(hs, o_s, p_sample[1].reshape(bs, -1)).reshape(bs, ss, d)

    return (y_prompt, y_sample,
            c_p, n_p, m_p.reshape(bp, ML_HEADS),
            sk.reshape(bp, sp, SB_HEADS, SB_DH), sv.reshape(bp, sp, SB_HEADS, SB_DH),
            k1.reshape(bp, sp, DF_HEADS, DF_V), v1.reshape(bp, sp, DF_HEADS, DF_V),
            c_s, n_s, m_s,
            sk_s.reshape(bs, ss, SB_HEADS, SB_DH), sv_s.reshape(bs, ss, SB_HEADS, SB_DH),
            k1s.reshape(bs, ss, DF_HEADS, DF_V), v1s.reshape(bs, ss, DF_HEADS, DF_V))
```
